```python
import math
import jax, jax.numpy as jnp
from jax import lax
import numpy as np

D_MODEL = 2048
BATCH = 1
SEQ = 16384
DEPTH = 1
DEC_BATCH = 16
DEC_SEQ = 32
PAST_LEN = 4096

CHUNK = 64
HEAD_DIM = 128
POOL_WIDTH = D_MODEL // 4
POOL_WINDOWS = (2, 4, 8, 16)
POOL_GROUPS = len(POOL_WINDOWS)
POOL_GROUP_DIM = POOL_WIDTH // POOL_GROUPS
POOL_HIST = max(POOL_WINDOWS) - 1
SB_WIDTH = D_MODEL // 2
SB_HEADS = SB_WIDTH // HEAD_DIM
MEM_WIDTH = D_MODEL // 4
MEM_HEADS = MEM_WIDTH // HEAD_DIM
N_MEM = 256
IN_WIDTH = POOL_WIDTH + 3 * SB_WIDTH + MEM_WIDTH
Q_BLOCK = 128
PEER_HEADS = 8
N_KEYS = 128
N_EXPERTS = N_KEYS * N_KEYS
PEER_QDIM = 256
PEER_HALF = PEER_QDIM // 2
PEER_TOPK = 16
TOKEN_BLOCK = 128
ALPHA = (2.0 * DEPTH) ** 0.25
DN_BETA = (8.0 * DEPTH) ** -0.25
LN_EPS = 1e-5

kernel_name = 'stream_pool_stickbreak_peer_step'


def layer_norm(x, g, b):
    xf = x.astype(jnp.float32)
    mu = jnp.mean(xf, axis=-1, keepdims=True)
    var = jnp.mean(jnp.square(xf - mu), axis=-1, keepdims=True)
    return ((xf - mu) * lax.rsqrt(var + LN_EPS) * g + b).astype(x.dtype)


def project_in(x, w_in):
    B, T, _ = x.shape
    y = x @ w_in
    o1 = POOL_WIDTH
    o2 = o1 + SB_WIDTH
    o3 = o2 + SB_WIDTH
    o4 = o3 + SB_WIDTH
    p = y[..., :o1]
    q = y[..., o1:o2].reshape(B, T, SB_HEADS, HEAD_DIM)
    k = y[..., o2:o3].reshape(B, T, SB_HEADS, HEAD_DIM)
    v = y[..., o3:o4].reshape(B, T, SB_HEADS, HEAD_DIM)
    qm = y[..., o4:].reshape(B, T, MEM_HEADS, HEAD_DIM)
    return p, q, k, v, qm


def pool_mixer(p, hist, pos0, w_pool, pool_scale):
    B, T, _ = p.shape
    buf = jnp.concatenate([hist.astype(p.dtype), p], axis=1)
    cs = jnp.pad(jnp.cumsum(buf.astype(jnp.float32), axis=1), ((0, 0), (1, 0), (0, 0)))
    pos = pos0 + jnp.arange(T, dtype=jnp.int32)
    outs = []
    for g, w in enumerate(POOL_WINDOWS):
        sl = slice(g * POOL_GROUP_DIM, (g + 1) * POOL_GROUP_DIM)
        win = cs[:, POOL_HIST + 1:POOL_HIST + 1 + T, sl] - cs[:, POOL_HIST + 1 - w:POOL_HIST + 1 - w + T, sl]
        cnt = jnp.minimum(pos + 1, w).astype(jnp.float32)[None, :, None]
        outs.append(win / cnt - p[..., sl].astype(jnp.float32))
    pooled = jnp.stack(outs, axis=2)
    mixed = jnp.einsum('btgc,gcd->btgd', pooled, w_pool).reshape(B, T, POOL_WIDTH) * pool_scale
    return mixed.astype(p.dtype), buf[:, -POOL_HIST:]


def stick_breaking(q, k, v, q_pos, k_pos):
    z = jnp.einsum('bqhd,bkhd->bhqk', q, k, preferred_element_type=jnp.float32) / math.sqrt(HEAD_DIM)
    mask = k_pos[None, :] < q_pos[:, None]
    log_keep = jnp.where(mask, jax.nn.log_sigmoid(-z), 0.0)
    between = lax.cumsum(log_keep, axis=3, reverse=True) - log_keep
    a = jnp.where(mask, jnp.exp(jax.nn.log_sigmoid(z) + between), 0.0)
    return jnp.einsum('bhqk,bkhd->bqhd', a.astype(v.dtype), v)


def stick_breaking_prompt(q, k, v):
    B, T, H, D = q.shape
    nb = T // Q_BLOCK
    qb = q.reshape(B, nb, Q_BLOCK, H, D).transpose(1, 0, 2, 3, 4)
    qpos = jnp.arange(T, dtype=jnp.int32).reshape(nb, Q_BLOCK)
    kpos = jnp.arange(T, dtype=jnp.int32)
    out = lax.map(lambda a: stick_breaking(a[0], k, v, a[1], kpos), (qb, qpos))
    return out.transpose(1, 0, 2, 3, 4).reshape(B, T, H, D)


def memory_kv(mem, w_mem_kv):
    B = mem.shape[0]
    kv = mem @ w_mem_kv
    mk = kv[..., :MEM_WIDTH].reshape(B, N_MEM, MEM_HEADS, HEAD_DIM)
    mv = kv[..., MEM_WIDTH:].reshape(B, N_MEM, MEM_HEADS, HEAD_DIM)
    return mk, mv


def mem_attend(qm, mk, mv):
    s = jnp.einsum('bqhd,bmhd->bhqm', qm, mk, preferred_element_type=jnp.float32) / math.sqrt(HEAD_DIM)
    pr = jax.nn.softmax(s, axis=-1)
    return jnp.einsum('bhqm,bmhd->bqhd', pr.astype(mv.dtype), mv)


def peer_block(x, wq, sk1, sk2, pu, pv):
    N = x.shape[0]
    q = (x @ wq).reshape(N, PEER_HEADS, PEER_QDIM)
    s1 = jnp.einsum('nhc,hkc->nhk', q[..., :PEER_HALF], sk1, preferred_element_type=jnp.float32)
    s2 = jnp.einsum('nhc,hkc->nhk', q[..., PEER_HALF:], sk2, preferred_element_type=jnp.float32)
    t1, i1 = lax.top_k(s1, PEER_TOPK)
    t2, i2 = lax.top_k(s2, PEER_TOPK)
    cand_s = (t1[..., :, None] + t2[..., None, :]).reshape(N, PEER_HEADS, PEER_TOPK * PEER_TOPK)
    cand_i = (i1[..., :, None] * N_KEYS + i2[..., None, :]).reshape(N, PEER_HEADS, PEER_TOPK * PEER_TOPK)
    top_s, top_j = lax.top_k(cand_s, PEER_TOPK)
    idx = jnp.take_along_axis(cand_i, top_j, axis=-1)
    gate = jax.nn.softmax(top_s, axis=-1)
    act = jax.nn.gelu(jnp.einsum('nd,nhkd->nhk', x, pu[idx]), approximate=False)
    return jnp.einsum('nhk,nhkd->nd', (gate * act).astype(pv.dtype), pv[idx])


def peer(x, wq, sk1, sk2, pu, pv):
    N = x.shape[0]
    if N > TOKEN_BLOCK and N % TOKEN_BLOCK == 0:
        xb = x.reshape(N // TOKEN_BLOCK, TOKEN_BLOCK, D_MODEL)
        return lax.map(lambda a: peer_block(a, wq, sk1, sk2, pu, pv), xb).reshape(N, D_MODEL)
    return peer_block(x, wq, sk1, sk2, pu, pv)


def layer_tail(x, pool_o, sb_o, mem_o, w_out, ln1_g, ln1_b, wq, sk1, sk2, pu, pv, ln2_g, ln2_b):
    B, T, _ = x.shape
    mixed = jnp.concatenate([pool_o, sb_o.reshape(B, T, SB_WIDTH), mem_o.reshape(B, T, MEM_WIDTH)], axis=-1) @ w_out
    h = layer_norm(ALPHA * x + mixed, ln1_g, ln1_b)
    f = peer(h.reshape(B * T, D_MODEL), wq, sk1, sk2, pu, pv).reshape(B, T, D_MODEL)
    return layer_norm(ALPHA * h + f.astype(h.dtype), ln2_g, ln2_b)


def setup_inputs(seed: int = 0) -> dict:
    key = jax.random.key(seed)
    ks = jax.random.split(key, 32)
    nrm = lambda k, shape, s: jax.random.normal(k, shape, jnp.float32) * s
    dsc = D_MODEL ** -0.5
    w_in = jnp.concatenate([
        nrm(ks[0], (DEPTH, D_MODEL, POOL_WIDTH + 2 * SB_WIDTH), dsc),
        nrm(ks[1], (DEPTH, D_MODEL, SB_WIDTH), dsc * DN_BETA),
        nrm(ks[2], (DEPTH, D_MODEL, MEM_WIDTH), dsc)], axis=-1)
    w_mem_kv = jnp.concatenate([
        nrm(ks[3], (DEPTH, D_MODEL, MEM_WIDTH), dsc),
        nrm(ks[4], (DEPTH, D_MODEL, MEM_WIDTH), dsc * DN_BETA)], axis=-1)
    return {
        'x_prompt': nrm(ks[5], (BATCH, SEQ, D_MODEL), 1.0),
        'x_sample': nrm(ks[6], (DEC_BATCH, DEC_SEQ, D_MODEL), 1.0),
        'state_pool': nrm(ks[7], (DEPTH, DEC_BATCH, POOL_HIST, POOL_WIDTH), 1.0),
        'cache_sb_k': nrm(ks[8], (DEPTH, DEC_BATCH, PAST_LEN, SB_HEADS, HEAD_DIM), 1.0),
        'cache_sb_v': nrm(ks[9], (DEPTH, DEC_BATCH, PAST_LEN, SB_HEADS, HEAD_DIM), DN_BETA),
        'cache_mem_k': nrm(ks[10], (DEPTH, DEC_BATCH, N_MEM, MEM_HEADS, HEAD_DIM), 1.0),
        'cache_mem_v': nrm(ks[11], (DEPTH, DEC_BATCH, N_MEM, MEM_HEADS, HEAD_DIM), DN_BETA),
        'mem_prompt': nrm(ks[12], (BATCH, N_MEM, D_MODEL), 1.0),
        'w_in': w_in,
        'w_pool': nrm(ks[13], (DEPTH, POOL_GROUPS, POOL_GROUP_DIM, POOL_GROUP_DIM), POOL_GROUP_DIM ** -0.5),
        'pool_scale': 1.0 + nrm(ks[14], (DEPTH, POOL_WIDTH), 0.02),
        'w_mem_kv': w_mem_kv,
        'w_out': nrm(ks[15], (DEPTH, D_MODEL, D_MODEL), dsc * DN_BETA),
        'ln1_g': 1.0 + nrm(ks[16], (DEPTH, D_MODEL), 0.02),
        'ln1_b': nrm(ks[17], (DEPTH, D_MODEL), 0.02),
        'peer_wq': nrm(ks[18], (DEPTH, D_MODEL, PEER_HEADS * PEER_QDIM), dsc),
        'peer_subkey1': nrm(ks[19], (DEPTH, PEER_HEADS, N_KEYS, PEER_HALF), PEER_HALF ** -0.5),
        'peer_subkey2': nrm(ks[20], (DEPTH, PEER_HEADS, N_KEYS, PEER_HALF), PEER_HALF ** -0.5),
        'peer_u': nrm(ks[21], (DEPTH, N_EXPERTS, D_MODEL), dsc),
        'peer_v': nrm(ks[22], (DEPTH, N_EXPERTS, D_MODEL), DN_BETA),
        'ln2_g': 1.0 + nrm(ks[23], (DEPTH, D_MODEL), 0.02),
        'ln2_b': nrm(ks[24], (DEPTH, D_MODEL), 0.02),
    }


def reference(x_prompt, x_sample, state_pool, cache_sb_k, cache_sb_v, cache_mem_k, cache_mem_v,
              mem_prompt, w_in, w_pool, pool_scale, w_mem_kv, w_out, ln1_g, ln1_b,
              peer_wq, peer_subkey1, peer_subkey2, peer_u, peer_v, ln2_g, ln2_b):
    xp, xs = x_prompt, x_sample
    B = xp.shape[0]
    Td = xs.shape[1]
    pool_p, sbk_p, sbv_p, mk_p, mv_p = [], [], [], [], []
    pool_s, sbk_s, sbv_s = [], [], []
    for l in range(DEPTH):
        tail_w = (w_out[l], ln1_g[l], ln1_b[l], peer_wq[l], peer_subkey1[l], peer_subkey2[l],
                  peer_u[l], peer_v[l], ln2_g[l], ln2_b[l])
        p, q, k, v, qm = project_in(xp, w_in[l])
        hist0 = jnp.zeros((B, POOL_HIST, POOL_WIDTH), p.dtype)
        pool_o, hist_p = pool_mixer(p, hist0, 0, w_pool[l], pool_scale[l])
        sb_o = stick_breaking_prompt(q, k, v)
        mk, mv = memory_kv(mem_prompt, w_mem_kv[l])
        mem_o = mem_attend(qm, mk, mv)
        xp_next = layer_tail(xp, pool_o, sb_o, mem_o, *tail_w)
        pool_p.append(hist_p)
        sbk_p.append(k)
        sbv_p.append(v)
        mk_p.append(mk)
        mv_p.append(mv)
        p, q, k, v, qm = project_in(xs, w_in[l])
        pool_o, hist_s = pool_mixer(p, state_pool[l], PAST_LEN, w_pool[l], pool_scale[l])
        keys = jnp.concatenate([cache_sb_k[l].astype(k.dtype), k], axis=1)
        vals = jnp.concatenate([cache_sb_v[l].astype(v.dtype), v], axis=1)
        q_pos = PAST_LEN + jnp.arange(Td, dtype=jnp.int32)
        k_pos = jnp.arange(PAST_LEN + Td, dtype=jnp.int32)
        sb_o = stick_breaking(q, keys, vals, q_pos, k_pos)
        mem_o = mem_attend(qm, cache_mem_k[l], cache_mem_v[l])
        xs_next = layer_tail(xs, pool_o, sb_o, mem_o, *tail_w)
        pool_s.append(hist_s)
        sbk_s.append(k)
        sbv_s.append(v)
        xp, xs = xp_next, xs_next
    return (xp, xs,
            jnp.stack(pool_p), jnp.stack(sbk_p), jnp.stack(sbv_p), jnp.stack(mk_p), jnp.stack(mv_p),
            jnp.stack(pool_s), jnp.stack(sbk_s), jnp.stack(sbv_s))
```

```python
import functools
import math

import jax
import jax.numpy as jnp
from jax import lax
from jax.experimental import pallas as pl
from jax.experimental.pallas import tpu as pltpu

_BF = jnp.bfloat16
_F32 = jnp.float32

D_MODEL = 2048
HEAD_DIM = 128
LANES = 128
POOL_WIDTH = D_MODEL // 4
POOL_WINDOWS = (2, 4, 8, 16)
POOL_GROUP_DIM = POOL_WIDTH // len(POOL_WINDOWS)
POOL_HIST = max(POOL_WINDOWS) - 1
POOL_HALO = POOL_HIST + 1
SB_WIDTH = D_MODEL // 2
SB_HEADS = SB_WIDTH // HEAD_DIM
MEM_WIDTH = D_MODEL // 4
MEM_HEADS = MEM_WIDTH // HEAD_DIM
IN_WIDTH = POOL_WIDTH + 3 * SB_WIDTH + MEM_WIDTH
PEER_HEADS = 8
N_KEYS = 128
N_EXPERTS = N_KEYS * N_KEYS
PEER_QDIM = 256
PEER_HALF = PEER_QDIM // 2
PEER_TOPK = 16
LN_EPS = 1e-5
ATT_SCALE = 1.0 / math.sqrt(HEAD_DIM)
SB_ZERO_LOG = -104.0
NOT_RANKED = 99.0

VMEM_LIMIT = 56 * 1024 * 1024


def _params(*sem):
    return pltpu.CompilerParams(dimension_semantics=sem, vmem_limit_bytes=VMEM_LIMIT)


def _nt_dot(a, b):
    return lax.dot_general(a, b, (((1,), (1,)), ((), ())), preferred_element_type=_F32)


def _dot(a, b):
    return jnp.dot(a, b, preferred_element_type=_F32)


def _resident(shape):
    nd = len(shape)
    return pl.BlockSpec(shape, lambda *_: (0,) * nd, pipeline_mode=pl.Buffered(1))


def _proj_body(x_ref, w_ref, *o_refs, plan):
    xb = x_ref[...].astype(_BF)
    n = 0
    for a, b, dtypes in plan:
        y = _dot(xb, w_ref[:, a:b])
        for dt in dtypes:
            o_refs[n][...] = y.astype(dt)
            n += 1


def _project(x, w_bf, plan, tm):
    t, k = x.shape
    n = w_bf.shape[1]
    shapes, specs = [], []
    for a, b, dtypes in plan:
        for dt in dtypes:
            shapes.append(jax.ShapeDtypeStruct((t, b - a), dt))
            specs.append(pl.BlockSpec((tm, b - a), lambda i: (i, 0)))
    return pl.pallas_call(
        functools.partial(_proj_body, plan=plan),
        out_shape=shapes,
        grid=(t // tm,),
        in_specs=[pl.BlockSpec((tm, k), lambda i: (i, 0)), _resident((k, n))],
        out_specs=specs,
        compiler_params=_params("parallel"),
        name="project",
    )(x, w_bf)


def _pool_body(p_ref, prev_ref, hist_ref, wp_ref, sc_ref, o_ref, buf_ref, *, tt, pos0):
    i = pl.program_id(1)
    buf_ref[0:POOL_HALO, :] = jnp.where(i == 0, hist_ref[0], prev_ref[0])
    buf_ref[POOL_HALO:POOL_HALO + tt, :] = p_ref[0]
    pos = pos0 + i * tt + lax.broadcasted_iota(jnp.int32, (tt, POOL_GROUP_DIM), 0)
    for g, w in enumerate(POOL_WINDOWS):
        cols = slice(g * POOL_GROUP_DIM, (g + 1) * POOL_GROUP_DIM)
        win = buf_ref[POOL_HALO - (w - 1):POOL_HALO - (w - 1) + tt, cols]
        for back in range(w - 2, -1, -1):
            win = win + buf_ref[POOL_HALO - back:POOL_HALO - back + tt, cols]
        cnt = jnp.minimum(pos + 1, w).astype(_F32)
        pooled = win / cnt - p_ref[0, :, cols]
        mixed = _dot(pooled.astype(_BF), wp_ref[g]) * sc_ref[:, cols]
        o_ref[0, :, cols] = mixed.astype(o_ref.dtype)


def _pool_mix(p, hist, wp_bf, scale, pos0, tt):
    b, t, c = p.shape
    hist16 = jnp.pad(hist, ((0, 0), (POOL_HALO - POOL_HIST, 0), (0, 0)))
    per = tt // POOL_HALO
    return pl.pallas_call(
        functools.partial(_pool_body, tt=tt, pos0=pos0),
        out_shape=jax.ShapeDtypeStruct((b, t, c), _BF),
        grid=(b, t // tt),
        in_specs=[
            pl.BlockSpec((1, tt, c), lambda bi, i: (bi, i, 0)),
            pl.BlockSpec((1, POOL_HALO, c), lambda bi, i: (bi, jnp.maximum(i * per - 1, 0), 0)),
            pl.BlockSpec((1, POOL_HALO, c), lambda bi, i: (bi, 0, 0)),
            _resident(wp_bf.shape),
            _resident(scale.shape),
        ],
        out_specs=pl.BlockSpec((1, tt, c), lambda bi, i: (bi, i, 0)),
        scratch_shapes=[pltpu.VMEM((POOL_HALO + tt, c), _F32)],
        compiler_params=_params("parallel", "parallel"),
        name="pool_mix",
    )(p, p, hist16, wp_bf, scale)


def _tri_ext(tk):
    j = lax.broadcasted_iota(jnp.int32, (tk, tk + LANES), 0)
    s = lax.broadcasted_iota(jnp.int32, (tk, tk + LANES), 1)
    return jnp.where((j > s) | (s >= tk), 1.0, 0.0).astype(_BF)


def _sb_block(q, kblk, vblk, c_b, tri, mask):
    tk = kblk.shape[0]
    z = _nt_dot(q, kblk) * ATT_SCALE
    softplus = jnp.maximum(z, 0.0) + jnp.log1p(jnp.exp(-jnp.abs(z)))
    log_keep = -softplus
    if mask is not None:
        log_keep = jnp.where(mask, log_keep, 0.0)
    hi = log_keep.astype(_BF)
    lo = (log_keep - hi.astype(_F32)).astype(_BF)
    ext = _dot(hi, tri) + _dot(lo, tri)
    between, total = ext[:, :tk], ext[:, tk:]
    newer = c_b if tk == LANES else jnp.concatenate([c_b] * (tk // LANES), axis=1)
    a = jnp.exp((z - softplus) + between + newer)
    if mask is not None:
        a = jnp.where(mask, a, 0.0)
    return _dot(a.astype(_BF), vblk), c_b + total


def _causal_mask(tq, tk):
    row = lax.broadcasted_iota(jnp.int32, (tq, tk), 0)
    col = lax.broadcasted_iota(jnp.int32, (tq, tk), 1)
    return col < row


def _sb_prompt_body(q_ref, k_ref, v_ref, tri_ref, o_ref, acc_ref, c_ref, *, tq):
    i = pl.program_id(1)
    q = q_ref[...]
    tri = tri_ref[...]
    start = pl.multiple_of(i * tq, tq)
    pv, c = _sb_block(q, k_ref[pl.ds(start, tq), :], v_ref[pl.ds(start, tq), :],
                      jnp.zeros((tq, LANES), _F32), tri, _causal_mask(tq, tq))
    acc_ref[...] = pv
    c_ref[...] = c

    def cond(state):
        j, done = state
        return jnp.logical_and(j >= 0, jnp.logical_not(done))

    def body(state):
        j, _ = state
        st = pl.multiple_of(j * tq, tq)
        pv, c = _sb_block(q, k_ref[pl.ds(st, tq), :], v_ref[pl.ds(st, tq), :], c_ref[...], tri, None)
        acc_ref[...] += pv
        c_ref[...] = c
        return j - 1, jnp.max(c) < SB_ZERO_LOG

    lax.while_loop(cond, body, (i - 1, jnp.max(c) < SB_ZERO_LOG))
    o_ref[...] = acc_ref[...].astype(o_ref.dtype)


def _sb_prompt(q_bf, k_bf, v_bf, tq):
    t = q_bf.shape[0]
    tri = _tri_ext(tq)
    return pl.pallas_call(
        functools.partial(_sb_prompt_body, tq=tq),
        out_shape=jax.ShapeDtypeStruct((t, SB_WIDTH), _BF),
        grid=(SB_HEADS, t // tq),
        in_specs=[
            pl.BlockSpec((tq, HEAD_DIM), lambda h, i: (i, h)),
            pl.BlockSpec((t, HEAD_DIM), lambda h, i: (0, h)),
            pl.BlockSpec((t, HEAD_DIM), lambda h, i: (0, h)),
            _resident(tri.shape),
        ],
        out_specs=pl.BlockSpec((tq, HEAD_DIM), lambda h, i: (i, h)),
        scratch_shapes=[pltpu.VMEM((tq, HEAD_DIM), _F32), pltpu.VMEM((tq, LANES), _F32)],
        compiler_params=_params("parallel", "parallel"),
        name="sb_prompt",
    )(q_bf, k_bf, v_bf, tri)


def _sb_cached_body(q_ref, kn_ref, vn_ref, ck_ref, cv_ref, trin_ref, tric_ref, o_ref,
                    kbuf, vbuf, sem, acc_ref, c_ref, *, td, tn, tc, n_blocks):
    b = pl.program_id(0)
    mask = _causal_mask(td, tn)
    trin = trin_ref[...]
    for h in range(SB_HEADS):
        cols = slice(h * HEAD_DIM, (h + 1) * HEAD_DIM)
        pv, c = _sb_block(q_ref[:, cols], kn_ref[0, :, cols], vn_ref[0, :, cols],
                          jnp.zeros((td, LANES), _F32), trin, mask)
        acc_ref[h] = pv
        c_ref[h] = c

    def copies(j):
        rows = pl.ds(pl.multiple_of(j * tc, tc), tc)
        return (pltpu.make_async_copy(ck_ref.at[b, rows, :], kbuf, sem.at[0]),
                pltpu.make_async_copy(cv_ref.at[b, rows, :], vbuf, sem.at[1]))

    def cond(state):
        j, done = state
        return jnp.logical_and(j >= 0, jnp.logical_not(done))

    def body(state):
        j, _ = state
        ck, cv = copies(j)
        ck.start()
        cv.start()
        ck.wait()
        cv.wait()
        tric = tric_ref[...]
        top = jnp.float32(-jnp.inf)
        for h in range(SB_HEADS):
            cols = slice(h * HEAD_DIM, (h + 1) * HEAD_DIM)
            pv, c = _sb_block(q_ref[:, cols], kbuf[:, cols].astype(_BF), vbuf[:, cols].astype(_BF),
                              c_ref[h], tric, None)
            acc_ref[h] += pv
            c_ref[h] = c
            top = jnp.maximum(top, jnp.max(c))
        return j - 1, top < SB_ZERO_LOG

    lax.while_loop(cond, body, (jnp.int32(n_blocks - 1), jnp.max(c_ref[...]) < SB_ZERO_LOG))
    for h in range(SB_HEADS):
        o_ref[:, h * HEAD_DIM:(h + 1) * HEAD_DIM] = acc_ref[h].astype(o_ref.dtype)


def _sb_cached(q_bf, k_new, v_new, cache_k, cache_v, td, tc):
    nb, past, width = cache_k.shape
    tn = LANES
    pad = ((0, 0), (0, tn - td), (0, 0))
    kn, vn = jnp.pad(k_new, pad), jnp.pad(v_new, pad)
    trin, tric = _tri_ext(tn), _tri_ext(tc)
    return pl.pallas_call(
        functools.partial(_sb_cached_body, td=td, tn=tn, tc=tc, n_blocks=past // tc),
        out_shape=jax.ShapeDtypeStruct((nb * td, width), _BF),
        grid=(nb,),
        in_specs=[
            pl.BlockSpec((td, width), lambda b: (b, 0)),
            pl.BlockSpec((1, tn, width), lambda b: (b, 0, 0)),
            pl.BlockSpec((1, tn, width), lambda b: (b, 0, 0)),
            pl.BlockSpec(memory_space=pl.ANY),
            pl.BlockSpec(memory_space=pl.ANY),
            _resident(trin.shape),
            _resident(tric.shape),
        ],
        out_specs=pl.BlockSpec((td, width), lambda b: (b, 0)),
        scratch_shapes=[
            pltpu.VMEM((tc, width), _F32),
            pltpu.VMEM((tc, width), _F32),
            pltpu.SemaphoreType.DMA((2,)),
            pltpu.VMEM((SB_HEADS, td, HEAD_DIM), _F32),
            pltpu.VMEM((SB_HEADS, td, LANES), _F32),
        ],
        compiler_params=_params("arbitrary"),
        name="sb_cached",
    )(q_bf, kn, vn, cache_k, cache_v, trin, tric)


def _mem_body(q_ref, k_ref, v_ref, o_ref):
    for h in range(MEM_HEADS):
        cols = slice(h * HEAD_DIM, (h + 1) * HEAD_DIM)
        s = _nt_dot(q_ref[0, :, cols], k_ref[0, :, cols].astype(_BF)) * ATT_SCALE
        e = jnp.exp(s - jnp.max(s, axis=1, keepdims=True))
        pr = e / jnp.sum(e, axis=1, keepdims=True)
        o_ref[0, :, cols] = _dot(pr.astype(_BF), v_ref[0, :, cols].astype(_BF)).astype(o_ref.dtype)


def _mem_attend(qm_bf, mk, mv, tt):
    b, t, w = qm_bf.shape
    m = mk.shape[1]
    return pl.pallas_call(
        _mem_body,
        out_shape=jax.ShapeDtypeStruct((b, t, w), _BF),
        grid=(b, t // tt),
        in_specs=[
            pl.BlockSpec((1, tt, w), lambda bi, i: (bi, i, 0)),
            pl.BlockSpec((1, m, w), lambda bi, i: (bi, 0, 0)),
            pl.BlockSpec((1, m, w), lambda bi, i: (bi, 0, 0)),
        ],
        out_specs=pl.BlockSpec((1, tt, w), lambda bi, i: (bi, i, 0)),
        compiler_params=_params("parallel", "parallel"),
        name="mem_attend",
    )(qm_bf, mk, mv)


def _layer_norm(r, g, b):
    mu = jnp.mean(r, axis=-1, keepdims=True)
    d = r - mu
    var = jnp.mean(d * d, axis=-1, keepdims=True)
    return d * lax.rsqrt(var + LN_EPS) * g + b


def _tail_body(x_ref, pool_ref, sb_ref, mem_ref, wo_ref, g_ref, b_ref, wq_ref, h_ref, hb_ref, pq_ref, *, alpha):
    o1, o2 = POOL_WIDTH, POOL_WIDTH + SB_WIDTH
    mix = (_dot(pool_ref[...], wo_ref[0:o1, :]) + _dot(sb_ref[...], wo_ref[o1:o2, :])
           + _dot(mem_ref[...], wo_ref[o2:, :]))
    h = _layer_norm(alpha * x_ref[...] + mix, g_ref[...], b_ref[...])
    h_ref[...] = h
    hb = h.astype(_BF)
    hb_ref[...] = hb
    for hd in range(PEER_HEADS):
        pq_ref[hd] = _dot(hb, wq_ref[:, hd * PEER_QDIM:(hd + 1) * PEER_QDIM]).astype(_BF)


def _tail(x, pool_o, sb_o, mem_o, wo_bf, g, b, wq_bf, alpha, tm):
    t = x.shape[0]
    row = lambda width: pl.BlockSpec((tm, width), lambda i: (i, 0))
    return pl.pallas_call(
        functools.partial(_tail_body, alpha=alpha),
        out_shape=[
            jax.ShapeDtypeStruct((t, D_MODEL), _F32),
            jax.ShapeDtypeStruct((t, D_MODEL), _BF),
            jax.ShapeDtypeStruct((PEER_HEADS, t, PEER_QDIM), _BF),
        ],
        grid=(t // tm,),
        in_specs=[row(D_MODEL), row(POOL_WIDTH), row(SB_WIDTH), row(MEM_WIDTH),
                  _resident(wo_bf.shape), _resident(g.shape), _resident(b.shape), _resident(wq_bf.shape)],
        out_specs=[row(D_MODEL), row(D_MODEL),
                   pl.BlockSpec((PEER_HEADS, tm, PEER_QDIM), lambda i: (0, i, 0))],
        compiler_params=_params("parallel"),
        name="tail",
    )(x, pool_o, sb_o, mem_o, wo_bf, g, b, wq_bf)


def _top_ranked(x, k):
    rows = lax.broadcasted_iota(jnp.int32, x.shape, 0).astype(_F32)
    work = x
    rank = jnp.full(x.shape, NOT_RANKED, _F32)
    vals = []
    for r in range(k):
        m = jnp.max(work, axis=0, keepdims=True)
        first = jnp.min(jnp.where(work == m, rows, float(x.shape[0])), axis=0, keepdims=True)
        sel = rows == first
        rank = jnp.where(sel, float(r), rank)
        work = jnp.where(sel, -jnp.inf, work)
        vals.append(m)
    return rank, jnp.concatenate(vals, axis=0)


def _score_body(pq_ref, sk1_ref, sk2_ref, r2_ref, e2_ref, cnt_ref, c1_ref):
    k = PEER_TOPK
    arow = lax.broadcasted_iota(jnp.int32, (k, LANES), 0).astype(_F32)

    def head(h, carry):
        pq = pq_ref[h]
        s1 = _nt_dot(sk1_ref[h], pq[:, :PEER_HALF])
        s2 = _nt_dot(sk2_ref[h], pq[:, PEER_HALF:])
        rank1, t1 = _top_ranked(s1, k)
        rank2, t2 = _top_ranked(s2, k)
        taken = jnp.zeros((k, LANES), _F32)
        best0 = t1[0:1] + t2[0:1]
        z = jnp.zeros((1, LANES), _F32)
        for _ in range(k):
            nxt = jnp.full((k, LANES), -jnp.inf, _F32)
            for bcol in range(k):
                nxt = jnp.where(taken == float(bcol), t2[bcol:bcol + 1], nxt)
            front = t1 + nxt
            m = jnp.max(front, axis=0, keepdims=True)
            first = jnp.min(jnp.where(front == m, arow, float(k)), axis=0, keepdims=True)
            taken = taken + jnp.where(arow == first, 1.0, 0.0)
            z = z + jnp.exp(m - best0)
        cnt = jnp.zeros((N_KEYS, LANES), _F32)
        for a in range(k):
            cnt = jnp.where(rank1 == float(a), taken[a:a + 1], cnt)
        r2_ref[h] = rank2
        e2_ref[h] = jnp.exp(s2 - t2[0:1])
        cnt_ref[h] = cnt
        c1_ref[h] = jnp.exp(s1 - t1[0:1]) / z
        return carry

    lax.fori_loop(0, PEER_HEADS, head, 0)


def _score(pq, sk1_bf, sk2_bf):
    t = pq.shape[1]
    out = jax.ShapeDtypeStruct((PEER_HEADS, N_KEYS, t), _F32)
    spec = pl.BlockSpec((PEER_HEADS, N_KEYS, LANES), lambda i: (0, 0, i))
    return pl.pallas_call(
        _score_body,
        out_shape=[out] * 4,
        grid=(t // LANES,),
        in_specs=[pl.BlockSpec((PEER_HEADS, LANES, PEER_QDIM), lambda i: (0, i, 0)),
                  _resident(sk1_bf.shape), _resident(sk2_bf.shape)],
        out_specs=[spec] * 4,
        compiler_params=_params("parallel"),
        name="peer_score",
    )(pq, sk1_bf, sk2_bf)


def _gelu(a):
    return 0.5 * a * (1.0 + lax.erf(a * (1.0 / math.sqrt(2.0))))


def _expert_body(hb_ref, pu_ref, pvt_ref, r2_ref, e2_ref, cnt_ref, c1_ref, o_ref, acc_ref, w_ref, *, te):
    j = pl.program_id(1)

    @pl.when(j == 0)
    def _():
        acc_ref[...] = jnp.zeros_like(acc_ref)

    act = _gelu(_nt_dot(pu_ref[...], hb_ref[...]))
    for s in range(te // N_KEYS):
        i1 = j * (te // N_KEYS) + s
        gate = jnp.zeros((N_KEYS, hb_ref.shape[0]), _F32)
        for h in range(PEER_HEADS):
            thr = cnt_ref[h, pl.ds(i1, 1), :]
            c1 = c1_ref[h, pl.ds(i1, 1), :]
            gate = gate + jnp.where(r2_ref[h] < thr, e2_ref[h] * c1, 0.0)
        rows = slice(s * N_KEYS, (s + 1) * N_KEYS)
        w_ref[rows, :] = (gate * act[rows, :]).astype(_BF)
    acc_ref[...] += _dot(pvt_ref[...], w_ref[...])

    @pl.when(j == pl.num_programs(1) - 1)
    def _():
        o_ref[...] = acc_ref[...].T


def _experts(hb, pu_bf, pvt_bf, r2, e2, cnt, c1, tm, te):
    t = hb.shape[0]
    aux = pl.BlockSpec((PEER_HEADS, N_KEYS, tm), lambda i, j: (0, 0, i), pipeline_mode=pl.Buffered(1))
    return pl.pallas_call(
        functools.partial(_expert_body, te=te),
        out_shape=jax.ShapeDtypeStruct((t, D_MODEL), _F32),
        grid=(t // tm, N_EXPERTS // te),
        in_specs=[
            pl.BlockSpec((tm, D_MODEL), lambda i, j: (i, 0), pipeline_mode=pl.Buffered(1)),
            pl.BlockSpec((te, D_MODEL), lambda i, j: (j, 0)),
            pl.BlockSpec((D_MODEL, te), lambda i, j: (0, j)),
            aux, aux, aux, aux,
        ],
        out_specs=pl.BlockSpec((tm, D_MODEL), lambda i, j: (i, 0)),
        scratch_shapes=[pltpu.VMEM((D_MODEL, tm), _F32), pltpu.VMEM((te, tm), _BF)],
        compiler_params=_params("parallel", "arbitrary"),
        name="peer_experts",
    )(hb, pu_bf, pvt_bf, r2, e2, cnt, c1)


def _final_body(h_ref, f_ref, g_ref, b_ref, o_ref, *, alpha):
    o_ref[...] = _layer_norm(alpha * h_ref[...] + f_ref[...], g_ref[...], b_ref[...])


def _final_norm(h, f_all, row0, g, b, alpha, tm):
    t = h.shape[0]
    off = row0 // tm
    return pl.pallas_call(
        functools.partial(_final_body, alpha=alpha),
        out_shape=jax.ShapeDtypeStruct((t, D_MODEL), _F32),
        grid=(t // tm,),
        in_specs=[pl.BlockSpec((tm, D_MODEL), lambda i: (i, 0)),
                  pl.BlockSpec((tm, D_MODEL), lambda i: (i + off, 0)),
                  _resident(g.shape), _resident(b.shape)],
        out_specs=pl.BlockSpec((tm, D_MODEL), lambda i: (i, 0)),
        compiler_params=_params("parallel"),
        name="final_norm",
    )(h, f_all, g, b)


_IN_PLAN = (
    (0, POOL_WIDTH, (_F32,)),
    (POOL_WIDTH, POOL_WIDTH + SB_WIDTH, (_BF,)),
    (POOL_WIDTH + SB_WIDTH, POOL_WIDTH + 2 * SB_WIDTH, (_F32, _BF)),
    (POOL_WIDTH + 2 * SB_WIDTH, POOL_WIDTH + 3 * SB_WIDTH, (_F32, _BF)),
    (POOL_WIDTH + 3 * SB_WIDTH, IN_WIDTH, (_BF,)),
)
_MEM_PLAN = ((0, MEM_WIDTH, (_F32,)), (MEM_WIDTH, 2 * MEM_WIDTH, (_F32,)))


def kernel(x_prompt, x_sample, state_pool, cache_sb_k, cache_sb_v, cache_mem_k, cache_mem_v, mem_prompt, w_in, w_pool, pool_scale, w_mem_kv, w_out, ln1_g, ln1_b, peer_wq, peer_subkey1, peer_subkey2, peer_u, peer_v, ln2_g, ln2_b):
    depth = w_in.shape[0]
    bp, seq, _ = x_prompt.shape
    bs, dec, _ = x_sample.shape
    past = cache_sb_k.shape[2]
    n_mem = mem_prompt.shape[1]
    assert bp == 1, "the prompt group is one stream"
    alpha = (2.0 * depth) ** 0.25
    tp, ts = seq * bp, bs * dec

    xp = x_prompt.reshape(tp, D_MODEL)
    xs = x_sample.reshape(ts, D_MODEL)
    outs = {k: [] for k in ("pool_p", "sbk_p", "sbv_p", "mk_p", "mv_p", "pool_s", "sbk_s", "sbv_s")}
    for l in range(depth):
        w_in_bf = w_in[l].astype(_BF)
        wp_bf = w_pool[l].astype(_BF)
        scale = pool_scale[l].reshape(1, POOL_WIDTH)
        wo_bf = w_out[l].astype(_BF)
        wq_bf = peer_wq[l].astype(_BF)
        sk1_bf = peer_subkey1[l].astype(_BF)
        sk2_bf = peer_subkey2[l].astype(_BF)
        pu_bf = peer_u[l].astype(_BF)
        pvt_bf = peer_v[l].T.astype(_BF)
        g1, b1 = ln1_g[l].reshape(1, D_MODEL), ln1_b[l].reshape(1, D_MODEL)
        g2, b2 = ln2_g[l].reshape(1, D_MODEL), ln2_b[l].reshape(1, D_MODEL)

        p, q, k, kb, v, vb, qm = _project(xp, w_in_bf, _IN_PLAN, 256)
        p3 = p.reshape(bp, seq, POOL_WIDTH)
        pool_o = _pool_mix(p3, jnp.zeros((bp, POOL_HIST, POOL_WIDTH), _F32), wp_bf, scale, 0, 512)
        sb_o = _sb_prompt(q, kb, vb, 256)
        mk, mv = _project(mem_prompt.reshape(bp * n_mem, D_MODEL), w_mem_kv[l].astype(_BF), _MEM_PLAN, 256)
        mem_o = _mem_attend(qm.reshape(bp, seq, MEM_WIDTH), mk.reshape(bp, n_mem, MEM_WIDTH),
                            mv.reshape(bp, n_mem, MEM_WIDTH), 512)
        h_p, hb_p, pq_p = _tail(xp, pool_o.reshape(tp, POOL_WIDTH), sb_o, mem_o.reshape(tp, MEM_WIDTH),
                                wo_bf, g1, b1, wq_bf, alpha, 256)
        outs["pool_p"].append(p3[:, seq - POOL_HIST:])
        outs["sbk_p"].append(k.reshape(bp, seq, SB_HEADS, HEAD_DIM))
        outs["sbv_p"].append(v.reshape(bp, seq, SB_HEADS, HEAD_DIM))
        outs["mk_p"].append(mk.reshape(bp, n_mem, MEM_HEADS, HEAD_DIM))
        outs["mv_p"].append(mv.reshape(bp, n_mem, MEM_HEADS, HEAD_DIM))

        p, q, k, kb, v, vb, qm = _project(xs, w_in_bf, _IN_PLAN, 256)
        p3 = p.reshape(bs, dec, POOL_WIDTH)
        pool_o = _pool_mix(p3, state_pool[l], wp_bf, scale, past, dec)
        sb_o = _sb_cached(q, kb.reshape(bs, dec, SB_WIDTH), vb.reshape(bs, dec, SB_WIDTH),
                          cache_sb_k[l].reshape(bs, past, SB_WIDTH), cache_sb_v[l].reshape(bs, past, SB_WIDTH),
                          dec, 256)
        mem_o = _mem_attend(qm.reshape(bs, dec, MEM_WIDTH), cache_mem_k[l].reshape(bs, n_mem, MEM_WIDTH),
                            cache_mem_v[l].reshape(bs, n_mem, MEM_WIDTH), dec)
        h_s, hb_s, pq_s = _tail(xs, pool_o.reshape(ts, POOL_WIDTH), sb_o, mem_o.reshape(ts, MEM_WIDTH),
                                wo_bf, g1, b1, wq_bf, alpha, 256)
        hist_s = jnp.concatenate([state_pool[l], p3], axis=1)[:, -POOL_HIST:]
        outs["pool_s"].append(hist_s)
        outs["sbk_s"].append(k.reshape(bs, dec, SB_HEADS, HEAD_DIM))
        outs["sbv_s"].append(v.reshape(bs, dec, SB_HEADS, HEAD_DIM))

        hb = jnp.concatenate([hb_p, hb_s], axis=0)
        pq = jnp.concatenate([pq_p, pq_s], axis=1)
        r2, e2, cnt, c1 = _score(pq, sk1_bf, sk2_bf)
        f = _experts(hb, pu_bf, pvt_bf, r2, e2, cnt, c1, 512, 512)
        xp = _final_norm(h_p, f, 0, g2, b2, alpha, 512)
        xs = _final_norm(h_s, f, tp, g2, b2, alpha, 512)

    st = lambda name: jnp.stack(outs[name])
    return (xp.reshape(bp, seq, D_MODEL), xs.reshape(bs, dec, D_MODEL),
            st("pool_p"), st("sbk_p"), st("sbv_p"), st("mk_p"), st("mv_p"),
            st("pool_s"), st("sbk_s"), st("sbv_s"))
```

```python
import functools
import math

import jax
import jax.numpy as jnp
from jax import lax
from jax.experimental import pallas as pl
from jax.experimental.pallas import tpu as pltpu

_BF = jnp.bfloat16
_F32 = jnp.float32

D_MODEL = 2048
HEAD_DIM = 128
LANES = 128
POOL_WIDTH = D_MODEL // 4
POOL_WINDOWS = (2, 4, 8, 16)
POOL_GROUP_DIM = POOL_WIDTH // len(POOL_WINDOWS)
POOL_HIST = max(POOL_WINDOWS) - 1
POOL_HALO = POOL_HIST + 1
SB_WIDTH = D_MODEL // 2
SB_HEADS = SB_WIDTH // HEAD_DIM
MEM_WIDTH = D_MODEL // 4
MEM_HEADS = MEM_WIDTH // HEAD_DIM
IN_WIDTH = POOL_WIDTH + 3 * SB_WIDTH + MEM_WIDTH
PEER_HEADS = 8
N_KEYS = 128
N_EXPERTS = N_KEYS * N_KEYS
PEER_QDIM = 256
PEER_HALF = PEER_QDIM // 2
PEER_TOPK = 16
LN_EPS = 1e-5
ATT_SCALE = 1.0 / math.sqrt(HEAD_DIM)
SB_ZERO_LOG = -104.0
NOT_RANKED = 99.0

VMEM_LIMIT = 56 * 1024 * 1024

ROW_TILE = 256
TOKEN_TILE = 512
EXPERT_TILE = 512
SB_TILE = 256
SB_HEADS_PER_STEP = 2


def _params(*sem):
    return pltpu.CompilerParams(dimension_semantics=sem, vmem_limit_bytes=VMEM_LIMIT)


def _nt_dot(a, b):
    return lax.dot_general(a, b, (((1,), (1,)), ((), ())), preferred_element_type=_F32)


def _dot(a, b):
    return jnp.dot(a, b, preferred_element_type=_F32)


def _resident(shape):
    nd = len(shape)
    return pl.BlockSpec(shape, lambda *_: (0,) * nd, pipeline_mode=pl.Buffered(1))


def _proj_body(x_ref, w_ref, *o_refs, plan):
    xb = x_ref[...].astype(_BF)
    n = 0
    for a, b, dtypes in plan:
        y = _dot(xb, w_ref[:, a:b])
        for dt in dtypes:
            o_refs[n][...] = y.astype(dt)
            n += 1


def _project(x, w_bf, plan, tm):
    t, k = x.shape
    n = w_bf.shape[1]
    shapes, specs = [], []
    for a, b, dtypes in plan:
        for dt in dtypes:
            shapes.append(jax.ShapeDtypeStruct((t, b - a), dt))
            specs.append(pl.BlockSpec((tm, b - a), lambda i: (i, 0)))
    return pl.pallas_call(
        functools.partial(_proj_body, plan=plan),
        out_shape=shapes,
        grid=(t // tm,),
        in_specs=[pl.BlockSpec((tm, k), lambda i: (i, 0)), _resident((k, n))],
        out_specs=specs,
        compiler_params=_params("parallel"),
        name="project",
    )(x, w_bf)


def _pool_body(p_ref, prev_ref, hist_ref, wp_ref, sc_ref, o_ref, buf_ref, *, tt, pos0):
    i = pl.program_id(1)
    buf_ref[0:POOL_HALO, :] = jnp.where(i == 0, hist_ref[0], prev_ref[0])
    buf_ref[POOL_HALO:POOL_HALO + tt, :] = p_ref[0]
    pos = pos0 + i * tt + lax.broadcasted_iota(jnp.int32, (tt, POOL_GROUP_DIM), 0)
    for g, w in enumerate(POOL_WINDOWS):
        cols = slice(g * POOL_GROUP_DIM, (g + 1) * POOL_GROUP_DIM)
        win = buf_ref[POOL_HALO - (w - 1):POOL_HALO - (w - 1) + tt, cols]
        for back in range(w - 2, -1, -1):
            win = win + buf_ref[POOL_HALO - back:POOL_HALO - back + tt, cols]
        cnt = jnp.minimum(pos + 1, w).astype(_F32)
        pooled = win / cnt - p_ref[0, :, cols]
        mixed = _dot(pooled.astype(_BF), wp_ref[g]) * sc_ref[:, cols]
        o_ref[0, :, cols] = mixed.astype(o_ref.dtype)


def _pool_mix(p, hist, wp_bf, scale, pos0, tt):
    b, t, c = p.shape
    hist16 = jnp.pad(hist, ((0, 0), (POOL_HALO - POOL_HIST, 0), (0, 0)))
    per = tt // POOL_HALO
    return pl.pallas_call(
        functools.partial(_pool_body, tt=tt, pos0=pos0),
        out_shape=jax.ShapeDtypeStruct((b, t, c), _BF),
        grid=(b, t // tt),
        in_specs=[
            pl.BlockSpec((1, tt, c), lambda bi, i: (bi, i, 0)),
            pl.BlockSpec((1, POOL_HALO, c), lambda bi, i: (bi, jnp.maximum(i * per - 1, 0), 0)),
            pl.BlockSpec((1, POOL_HALO, c), lambda bi, i: (bi, 0, 0)),
            _resident(wp_bf.shape),
            _resident(scale.shape),
        ],
        out_specs=pl.BlockSpec((1, tt, c), lambda bi, i: (bi, i, 0)),
        scratch_shapes=[pltpu.VMEM((POOL_HALO + tt, c), _F32)],
        compiler_params=_params("parallel", "parallel"),
        name="pool_mix",
    )(p, p, hist16, wp_bf, scale)


def _tri_ext(tk):
    j = lax.broadcasted_iota(jnp.int32, (tk, tk + LANES), 0)
    s = lax.broadcasted_iota(jnp.int32, (tk, tk + LANES), 1)
    return jnp.where((j > s) | (s >= tk), 1.0, 0.0).astype(_BF)


def _sb_block(q, kblk, vblk, c_b, tri, mask):
    tk = kblk.shape[0]
    z = _nt_dot(q, kblk) * ATT_SCALE
    softplus = jnp.maximum(z, 0.0) + jnp.log1p(jnp.exp(-jnp.abs(z)))
    log_keep = -softplus
    if mask is not None:
        log_keep = jnp.where(mask, log_keep, 0.0)
    hi = log_keep.astype(_BF)
    lo = (log_keep - hi.astype(_F32)).astype(_BF)
    ext = _dot(hi, tri) + _dot(lo, tri)
    between, total = ext[:, :tk], ext[:, tk:]
    newer = c_b if tk == LANES else jnp.concatenate([c_b] * (tk // LANES), axis=1)
    a = jnp.exp((z - softplus) + between + newer)
    if mask is not None:
        a = jnp.where(mask, a, 0.0)
    return _dot(a.astype(_BF), vblk), c_b + total


def _causal_mask(tq, tk):
    row = lax.broadcasted_iota(jnp.int32, (tq, tk), 0)
    col = lax.broadcasted_iota(jnp.int32, (tq, tk), 1)
    return col < row


def _sb_prompt_body(q_ref, k_ref, v_ref, tri_ref, o_ref, acc_ref, c_ref, *, tq, nh):
    i = pl.program_id(1)
    tri = tri_ref[...]
    heads = [slice(h * HEAD_DIM, (h + 1) * HEAD_DIM) for h in range(nh)]

    def step(j, mask, first):
        rows = pl.ds(pl.multiple_of(j * tq, tq), tq)
        top = None
        for h, cols in enumerate(heads):
            c_b = jnp.zeros((tq, LANES), _F32) if first else c_ref[h]
            pv, c = _sb_block(q_ref[:, cols], k_ref[rows, cols], v_ref[rows, cols], c_b, tri, mask)
            acc_ref[h] = pv if first else acc_ref[h] + pv
            c_ref[h] = c
            top = jnp.max(c) if top is None else jnp.maximum(top, jnp.max(c))
        return top < SB_ZERO_LOG

    def cond(state):
        j, done = state
        return jnp.logical_and(j >= 0, jnp.logical_not(done))

    def body(state):
        j, _ = state
        return j - 1, step(j, None, False)

    lax.while_loop(cond, body, (i - 1, step(i, _causal_mask(tq, tq), True)))
    for h, cols in enumerate(heads):
        o_ref[:, cols] = acc_ref[h].astype(o_ref.dtype)


def _sb_prompt(q_bf, k_bf, v_bf, tq, nh):
    t = q_bf.shape[0]
    tri = _tri_ext(tq)
    w = nh * HEAD_DIM
    return pl.pallas_call(
        functools.partial(_sb_prompt_body, tq=tq, nh=nh),
        out_shape=jax.ShapeDtypeStruct((t, SB_WIDTH), _BF),
        grid=(SB_HEADS // nh, t // tq),
        in_specs=[
            pl.BlockSpec((tq, w), lambda g, i: (i, g)),
            pl.BlockSpec((t, w), lambda g, i: (0, g)),
            pl.BlockSpec((t, w), lambda g, i: (0, g)),
            _resident(tri.shape),
        ],
        out_specs=pl.BlockSpec((tq, w), lambda g, i: (i, g)),
        scratch_shapes=[pltpu.VMEM((nh, tq, HEAD_DIM), _F32), pltpu.VMEM((nh, tq, LANES), _F32)],
        compiler_params=_params("parallel", "parallel"),
        name="sb_prompt",
    )(q_bf, k_bf, v_bf, tri)


def _sb_cached_body(q_ref, kn_ref, vn_ref, ck_ref, cv_ref, trin_ref, tric_ref, o_ref,
                    kbuf, vbuf, sem, acc_ref, c_ref, *, td, tn, tc, n_blocks):
    b = pl.program_id(0)
    mask = _causal_mask(td, tn)
    trin = trin_ref[...]
    for h in range(SB_HEADS):
        cols = slice(h * HEAD_DIM, (h + 1) * HEAD_DIM)
        pv, c = _sb_block(q_ref[:, cols], kn_ref[0, :, cols], vn_ref[0, :, cols],
                          jnp.zeros((td, LANES), _F32), trin, mask)
        acc_ref[h] = pv
        c_ref[h] = c

    def copies(j):
        rows = pl.ds(pl.multiple_of(j * tc, tc), tc)
        return (pltpu.make_async_copy(ck_ref.at[b, rows, :], kbuf, sem.at[0]),
                pltpu.make_async_copy(cv_ref.at[b, rows, :], vbuf, sem.at[1]))

    def cond(state):
        j, done = state
        return jnp.logical_and(j >= 0, jnp.logical_not(done))

    def body(state):
        j, _ = state
        ck, cv = copies(j)
        ck.start()
        cv.start()
        ck.wait()
        cv.wait()
        tric = tric_ref[...]
        top = jnp.float32(-jnp.inf)
        for h in range(SB_HEADS):
            cols = slice(h * HEAD_DIM, (h + 1) * HEAD_DIM)
            pv, c = _sb_block(q_ref[:, cols], kbuf[:, cols].astype(_BF), vbuf[:, cols].astype(_BF),
                              c_ref[h], tric, None)
            acc_ref[h] += pv
            c_ref[h] = c
            top = jnp.maximum(top, jnp.max(c))
        return j - 1, top < SB_ZERO_LOG

    lax.while_loop(cond, body, (jnp.int32(n_blocks - 1), jnp.max(c_ref[...]) < SB_ZERO_LOG))
    for h in range(SB_HEADS):
        o_ref[:, h * HEAD_DIM:(h + 1) * HEAD_DIM] = acc_ref[h].astype(o_ref.dtype)


def _sb_cached(q_bf, k_new, v_new, cache_k, cache_v, td, tc):
    nb, past, width = cache_k.shape
    tn = LANES
    pad = ((0, 0), (0, tn - td), (0, 0))
    kn, vn = jnp.pad(k_new, pad), jnp.pad(v_new, pad)
    trin, tric = _tri_ext(tn), _tri_ext(tc)
    return pl.pallas_call(
        functools.partial(_sb_cached_body, td=td, tn=tn, tc=tc, n_blocks=past // tc),
        out_shape=jax.ShapeDtypeStruct((nb * td, width), _BF),
        grid=(nb,),
        in_specs=[
            pl.BlockSpec((td, width), lambda b: (b, 0)),
            pl.BlockSpec((1, tn, width), lambda b: (b, 0, 0)),
            pl.BlockSpec((1, tn, width), lambda b: (b, 0, 0)),
            pl.BlockSpec(memory_space=pl.ANY),
            pl.BlockSpec(memory_space=pl.ANY),
            _resident(trin.shape),
            _resident(tric.shape),
        ],
        out_specs=pl.BlockSpec((td, width), lambda b: (b, 0)),
        scratch_shapes=[
            pltpu.VMEM((tc, width), _F32),
            pltpu.VMEM((tc, width), _F32),
            pltpu.SemaphoreType.DMA((2,)),
            pltpu.VMEM((SB_HEADS, td, HEAD_DIM), _F32),
            pltpu.VMEM((SB_HEADS, td, LANES), _F32),
        ],
        compiler_params=_params("arbitrary"),
        name="sb_cached",
    )(q_bf, kn, vn, cache_k, cache_v, trin, tric)


def _mem_body(q_ref, k_ref, v_ref, o_ref):
    for h in range(MEM_HEADS):
        cols = slice(h * HEAD_DIM, (h + 1) * HEAD_DIM)
        s = _nt_dot(q_ref[0, :, cols], k_ref[0, :, cols].astype(_BF)) * ATT_SCALE
        e = jnp.exp(s - jnp.max(s, axis=1, keepdims=True))
        pr = e / jnp.sum(e, axis=1, keepdims=True)
        o_ref[0, :, cols] = _dot(pr.astype(_BF), v_ref[0, :, cols].astype(_BF)).astype(o_ref.dtype)


def _mem_attend(qm_bf, mk, mv, tt):
    b, t, w = qm_bf.shape
    m = mk.shape[1]
    return pl.pallas_call(
        _mem_body,
        out_shape=jax.ShapeDtypeStruct((b, t, w), _BF),
        grid=(b, t // tt),
        in_specs=[
            pl.BlockSpec((1, tt, w), lambda bi, i: (bi, i, 0)),
            pl.BlockSpec((1, m, w), lambda bi, i: (bi, 0, 0)),
            pl.BlockSpec((1, m, w), lambda bi, i: (bi, 0, 0)),
        ],
        out_specs=pl.BlockSpec((1, tt, w), lambda bi, i: (bi, i, 0)),
        compiler_params=_params("parallel", "parallel"),
        name="mem_attend",
    )(qm_bf, mk, mv)


def _layer_norm(r, g, b):
    mu = jnp.mean(r, axis=-1, keepdims=True)
    d = r - mu
    var = jnp.mean(d * d, axis=-1, keepdims=True)
    return d * lax.rsqrt(var + LN_EPS) * g + b


def _tail_body(x_ref, pool_ref, sb_ref, mem_ref, wo_ref, g_ref, b_ref, wq_ref, h_ref, hb_ref, pq_ref, *, alpha):
    o1, o2 = POOL_WIDTH, POOL_WIDTH + SB_WIDTH
    mix = (_dot(pool_ref[...], wo_ref[0:o1, :]) + _dot(sb_ref[...], wo_ref[o1:o2, :])
           + _dot(mem_ref[...], wo_ref[o2:, :]))
    h = _layer_norm(alpha * x_ref[...] + mix, g_ref[...], b_ref[...])
    h_ref[...] = h
    hb = h.astype(_BF)
    hb_ref[...] = hb
    for hd in range(PEER_HEADS):
        pq_ref[hd] = _dot(hb, wq_ref[:, hd * PEER_QDIM:(hd + 1) * PEER_QDIM]).astype(_BF)


def _tail(x, pool_o, sb_o, mem_o, wo_bf, g, b, wq_bf, alpha, tm):
    t = x.shape[0]
    row = lambda width: pl.BlockSpec((tm, width), lambda i: (i, 0))
    return pl.pallas_call(
        functools.partial(_tail_body, alpha=alpha),
        out_shape=[
            jax.ShapeDtypeStruct((t, D_MODEL), _F32),
            jax.ShapeDtypeStruct((t, D_MODEL), _BF),
            jax.ShapeDtypeStruct((PEER_HEADS, t, PEER_QDIM), _BF),
        ],
        grid=(t // tm,),
        in_specs=[row(D_MODEL), row(POOL_WIDTH), row(SB_WIDTH), row(MEM_WIDTH),
                  _resident(wo_bf.shape), _resident(g.shape), _resident(b.shape), _resident(wq_bf.shape)],
        out_specs=[row(D_MODEL), row(D_MODEL),
                   pl.BlockSpec((PEER_HEADS, tm, PEER_QDIM), lambda i: (0, i, 0))],
        compiler_params=_params("parallel"),
        name="tail",
    )(x, pool_o, sb_o, mem_o, wo_bf, g, b, wq_bf)


def _top_ranked(x, k):
    rows = lax.broadcasted_iota(jnp.int32, x.shape, 0).astype(_F32)
    work = x
    rank = jnp.full(x.shape, NOT_RANKED, _F32)
    vals = []
    for r in range(k):
        m = jnp.max(work, axis=0, keepdims=True)
        first = jnp.min(jnp.where(work == m, rows, float(x.shape[0])), axis=0, keepdims=True)
        sel = rows == first
        rank = jnp.where(sel, float(r), rank)
        work = jnp.where(sel, -jnp.inf, work)
        vals.append(m)
    return rank, jnp.concatenate(vals, axis=0)


def _top_ranked_untied(x, k):
    work = x
    rank = jnp.full(x.shape, NOT_RANKED, _F32)
    vals = []
    for r in range(k):
        m = jnp.max(work, axis=0, keepdims=True)
        sel = work == m
        rank = jnp.where(sel, float(r), rank)
        work = jnp.where(sel, -jnp.inf, work)
        vals.append(m)
    return rank, jnp.concatenate(vals, axis=0)


def _all_ranked_once(rank, k):
    n = jnp.sum(jnp.where(rank < NOT_RANKED, 1.0, 0.0), axis=0, keepdims=True)
    return jnp.max(jnp.abs(n - float(k))) == 0.0


def _score_body(pqp_ref, pqs_ref, sk1_ref, sk2_ref, r2_ref, e2_ref, cnt_ref, c1_ref,
                rank1_s, t1_s, rank2_s, t2_s, *, prompt_steps):
    k = PEER_TOPK
    arow = lax.broadcasted_iota(jnp.int32, (k, LANES), 0).astype(_F32)
    from_prompt = pl.program_id(0) < prompt_steps

    def head(h, carry):
        pq = jnp.where(from_prompt, pqp_ref[h], pqs_ref[h])
        s1 = _nt_dot(sk1_ref[h], pq[:, :PEER_HALF])
        s2 = _nt_dot(sk2_ref[h], pq[:, PEER_HALF:])
        rank1_s[...], t1_s[...] = _top_ranked_untied(s1, k)
        rank2_s[...], t2_s[...] = _top_ranked_untied(s2, k)
        untied = jnp.logical_and(_all_ranked_once(rank1_s[...], k), _all_ranked_once(rank2_s[...], k))

        @pl.when(jnp.logical_not(untied))
        def _():
            rank1_s[...], t1_s[...] = _top_ranked(s1, k)
            rank2_s[...], t2_s[...] = _top_ranked(s2, k)

        rank1, t1, rank2, t2 = rank1_s[...], t1_s[...], rank2_s[...], t2_s[...]
        taken = jnp.zeros((k, LANES), _F32)
        best0 = t1[0:1] + t2[0:1]
        z = jnp.zeros((1, LANES), _F32)
        for _ in range(k):
            nxt = jnp.full((k, LANES), -jnp.inf, _F32)
            for bcol in range(k):
                nxt = jnp.where(taken == float(bcol), t2[bcol:bcol + 1], nxt)
            front = t1 + nxt
            m = jnp.max(front, axis=0, keepdims=True)
            first = jnp.min(jnp.where(front == m, arow, float(k)), axis=0, keepdims=True)
            taken = taken + jnp.where(arow == first, 1.0, 0.0)
            z = z + jnp.exp(m - best0)
        cnt = jnp.zeros((N_KEYS, LANES), _F32)
        for a in range(k):
            cnt = jnp.where(rank1 == float(a), taken[a:a + 1], cnt)
        r2_ref[h] = rank2.astype(r2_ref.dtype)
        e2_ref[h] = jnp.exp(s2 - t2[0:1]).astype(e2_ref.dtype)
        cnt_ref[h] = cnt
        c1_ref[h] = jnp.exp(s1 - t1[0:1]) / z
        return carry

    lax.fori_loop(0, PEER_HEADS, head, 0)


def _score(pq_p, pq_s, sk1_bf, sk2_bf):
    tp, ts = pq_p.shape[1], pq_s.shape[1]
    t = tp + ts
    np_ = tp // LANES
    out = lambda dt: jax.ShapeDtypeStruct((PEER_HEADS, N_KEYS, t), dt)
    spec = pl.BlockSpec((PEER_HEADS, N_KEYS, LANES), lambda i: (0, 0, i))
    return pl.pallas_call(
        functools.partial(_score_body, prompt_steps=np_),
        out_shape=[out(_BF), out(_BF), out(_F32), out(_F32)],
        grid=(t // LANES,),
        in_specs=[pl.BlockSpec((PEER_HEADS, LANES, PEER_QDIM), lambda i: (0, jnp.minimum(i, np_ - 1), 0)),
                  pl.BlockSpec((PEER_HEADS, LANES, PEER_QDIM), lambda i: (0, jnp.maximum(i - np_, 0), 0)),
                  _resident(sk1_bf.shape), _resident(sk2_bf.shape)],
        out_specs=[spec] * 4,
        scratch_shapes=[pltpu.VMEM((N_KEYS, LANES), _F32), pltpu.VMEM((PEER_TOPK, LANES), _F32),
                        pltpu.VMEM((N_KEYS, LANES), _F32), pltpu.VMEM((PEER_TOPK, LANES), _F32)],
        compiler_params=_params("parallel"),
        name="peer_score",
    )(pq_p, pq_s, sk1_bf, sk2_bf)


def _gelu(a):
    return 0.5 * a * (1.0 + lax.erf(a * (1.0 / math.sqrt(2.0))))


def _expert_body(hbp_ref, hbs_ref, pu_ref, pvt_ref, r2_ref, e2_ref, cnt_ref, c1_ref, o_ref,
                 hb_ref, acc_ref, w_ref, *, te, prompt_tiles):
    i = pl.program_id(0)
    j = pl.program_id(1)

    @pl.when(j == 0)
    def _():
        acc_ref[...] = jnp.zeros_like(acc_ref)

    @pl.when(jnp.logical_and(j == 0, i < prompt_tiles))
    def _():
        hb_ref[...] = hbp_ref[...]

    @pl.when(jnp.logical_and(j == 0, i >= prompt_tiles))
    def _():
        hb_ref[...] = hbs_ref[...]

    tm = hb_ref.shape[0]
    act = _gelu(_nt_dot(pu_ref[...], hb_ref[...]))
    zero = jnp.zeros((), _BF)
    for s in range(te // N_KEYS):
        i1 = j * (te // N_KEYS) + s
        gate = jnp.zeros((N_KEYS, tm), _BF)
        for h in range(PEER_HEADS):
            thr = cnt_ref[h, pl.ds(i1, 1), :].astype(_BF)
            c1 = c1_ref[h, pl.ds(i1, 1), :].astype(_BF)
            gate = gate + jnp.where(r2_ref[h] < thr, e2_ref[h] * c1, zero)
        rows = slice(s * N_KEYS, (s + 1) * N_KEYS)
        w_ref[rows, :] = gate * act[rows, :].astype(_BF)
    acc_ref[...] += _dot(pvt_ref[...], w_ref[...])

    @pl.when(j == pl.num_programs(1) - 1)
    def _():
        o_ref[...] = acc_ref[...].T


def _experts(hb_p, hb_s, pu_bf, pvt_bf, r2, e2, cnt, c1, tm, te):
    tp, ts = hb_p.shape[0], hb_s.shape[0]
    t = tp + ts
    npt = tp // tm
    aux = pl.BlockSpec((PEER_HEADS, N_KEYS, tm), lambda i, j: (0, 0, i), pipeline_mode=pl.Buffered(1))
    return pl.pallas_call(
        functools.partial(_expert_body, te=te, prompt_tiles=npt),
        out_shape=jax.ShapeDtypeStruct((t, D_MODEL), _F32),
        grid=(t // tm, N_EXPERTS // te),
        in_specs=[
            pl.BlockSpec((tm, D_MODEL), lambda i, j: (jnp.minimum(i, npt - 1), 0)),
            pl.BlockSpec((tm, D_MODEL), lambda i, j: (jnp.maximum(i - npt, 0), 0)),
            pl.BlockSpec((te, D_MODEL), lambda i, j: (j, 0)),
            pl.BlockSpec((None, D_MODEL, te), lambda i, j: (j, 0, 0)),
            aux, aux, aux, aux,
        ],
        out_specs=pl.BlockSpec((tm, D_MODEL), lambda i, j: (i, 0)),
        scratch_shapes=[pltpu.VMEM((tm, D_MODEL), _BF), pltpu.VMEM((D_MODEL, tm), _F32),
                        pltpu.VMEM((te, tm), _BF)],
        compiler_params=_params("parallel", "arbitrary"),
        name="peer_experts",
    )(hb_p, hb_s, pu_bf, pvt_bf, r2, e2, cnt, c1)


def _final_body(h_ref, f_ref, g_ref, b_ref, o_ref, *, alpha):
    o_ref[...] = _layer_norm(alpha * h_ref[...] + f_ref[...], g_ref[...], b_ref[...])


def _final_norm(h, f_all, row0, g, b, alpha, tm):
    t = h.shape[0]
    off = row0 // tm
    return pl.pallas_call(
        functools.partial(_final_body, alpha=alpha),
        out_shape=jax.ShapeDtypeStruct((t, D_MODEL), _F32),
        grid=(t // tm,),
        in_specs=[pl.BlockSpec((tm, D_MODEL), lambda i: (i, 0)),
                  pl.BlockSpec((tm, D_MODEL), lambda i: (i + off, 0)),
                  _resident(g.shape), _resident(b.shape)],
        out_specs=pl.BlockSpec((tm, D_MODEL), lambda i: (i, 0)),
        compiler_params=_params("parallel"),
        name="final_norm",
    )(h, f_all, g, b)


_IN_PLAN = (
    (0, POOL_WIDTH, (_F32,)),
    (POOL_WIDTH, POOL_WIDTH + SB_WIDTH, (_BF,)),
    (POOL_WIDTH + SB_WIDTH, POOL_WIDTH + 2 * SB_WIDTH, (_F32, _BF)),
    (POOL_WIDTH + 2 * SB_WIDTH, POOL_WIDTH + 3 * SB_WIDTH, (_F32, _BF)),
    (POOL_WIDTH + 3 * SB_WIDTH, IN_WIDTH, (_BF,)),
)
_MEM_PLAN = ((0, MEM_WIDTH, (_F32,)), (MEM_WIDTH, 2 * MEM_WIDTH, (_F32,)))


def kernel(x_prompt, x_sample, state_pool, cache_sb_k, cache_sb_v, cache_mem_k, cache_mem_v, mem_prompt, w_in, w_pool, pool_scale, w_mem_kv, w_out, ln1_g, ln1_b, peer_wq, peer_subkey1, peer_subkey2, peer_u, peer_v, ln2_g, ln2_b):
    depth = w_in.shape[0]
    bp, seq, _ = x_prompt.shape
    bs, dec, _ = x_sample.shape
    past = cache_sb_k.shape[2]
    n_mem = mem_prompt.shape[1]
    assert bp == 1, "the prompt group is one stream"
    alpha = (2.0 * depth) ** 0.25
    tp, ts = seq * bp, bs * dec

    xp = x_prompt.reshape(tp, D_MODEL)
    xs = x_sample.reshape(ts, D_MODEL)
    outs = {k: [] for k in ("pool_p", "sbk_p", "sbv_p", "mk_p", "mv_p", "pool_s", "sbk_s", "sbv_s")}
    def layer(a, l):
        return a.reshape(a.shape[1:]) if a.shape[0] == 1 else a[l]

    for l in range(depth):
        w_in_bf = layer(w_in, l).astype(_BF)
        wp_bf = layer(w_pool, l).astype(_BF)
        scale = layer(pool_scale, l).reshape(1, POOL_WIDTH)
        wo_bf = layer(w_out, l).astype(_BF)
        wq_bf = layer(peer_wq, l).astype(_BF)
        sk1_bf = layer(peer_subkey1, l).astype(_BF)
        sk2_bf = layer(peer_subkey2, l).astype(_BF)
        pu_bf = layer(peer_u, l).astype(_BF)
        pvt_bf = jnp.swapaxes(layer(peer_v, l).reshape(N_EXPERTS // EXPERT_TILE, EXPERT_TILE, D_MODEL),
                              1, 2).astype(_BF)
        g1, b1 = layer(ln1_g, l).reshape(1, D_MODEL), layer(ln1_b, l).reshape(1, D_MODEL)
        g2, b2 = layer(ln2_g, l).reshape(1, D_MODEL), layer(ln2_b, l).reshape(1, D_MODEL)
        hist_s0 = layer(state_pool, l)

        p, q, k, kb, v, vb, qm = _project(xp, w_in_bf, _IN_PLAN, ROW_TILE)
        p3 = p.reshape(bp, seq, POOL_WIDTH)
        pool_o = _pool_mix(p3, jnp.zeros((bp, POOL_HIST, POOL_WIDTH), _F32), wp_bf, scale, 0, TOKEN_TILE)
        sb_o = _sb_prompt(q, kb, vb, SB_TILE, SB_HEADS_PER_STEP)
        mk, mv = _project(mem_prompt.reshape(bp * n_mem, D_MODEL), layer(w_mem_kv, l).astype(_BF), _MEM_PLAN,
                          ROW_TILE)
        mem_o = _mem_attend(qm.reshape(bp, seq, MEM_WIDTH), mk.reshape(bp, n_mem, MEM_WIDTH),
                            mv.reshape(bp, n_mem, MEM_WIDTH), TOKEN_TILE)
        h_p, hb_p, pq_p = _tail(xp, pool_o.reshape(tp, POOL_WIDTH), sb_o, mem_o.reshape(tp, MEM_WIDTH),
                                wo_bf, g1, b1, wq_bf, alpha, ROW_TILE)
        outs["pool_p"].append(p3[:, seq - POOL_HIST:])
        outs["sbk_p"].append(k.reshape(bp, seq, SB_HEADS, HEAD_DIM))
        outs["sbv_p"].append(v.reshape(bp, seq, SB_HEADS, HEAD_DIM))
        outs["mk_p"].append(mk.reshape(bp, n_mem, MEM_HEADS, HEAD_DIM))
        outs["mv_p"].append(mv.reshape(bp, n_mem, MEM_HEADS, HEAD_DIM))

        p, q, k, kb, v, vb, qm = _project(xs, w_in_bf, _IN_PLAN, ROW_TILE)
        p3 = p.reshape(bs, dec, POOL_WIDTH)
        pool_o = _pool_mix(p3, hist_s0, wp_bf, scale, past, dec)
        sb_o = _sb_cached(q, kb.reshape(bs, dec, SB_WIDTH), vb.reshape(bs, dec, SB_WIDTH),
                          layer(cache_sb_k, l).reshape(bs, past, SB_WIDTH),
                          layer(cache_sb_v, l).reshape(bs, past, SB_WIDTH), dec, SB_TILE)
        mem_o = _mem_attend(qm.reshape(bs, dec, MEM_WIDTH), layer(cache_mem_k, l).reshape(bs, n_mem, MEM_WIDTH),
                            layer(cache_mem_v, l).reshape(bs, n_mem, MEM_WIDTH), dec)
        h_s, hb_s, pq_s = _tail(xs, pool_o.reshape(ts, POOL_WIDTH), sb_o, mem_o.reshape(ts, MEM_WIDTH),
                                wo_bf, g1, b1, wq_bf, alpha, ROW_TILE)
        hist_s = jnp.concatenate([hist_s0, p3], axis=1)[:, -POOL_HIST:]
        outs["pool_s"].append(hist_s)
        outs["sbk_s"].append(k.reshape(bs, dec, SB_HEADS, HEAD_DIM))
        outs["sbv_s"].append(v.reshape(bs, dec, SB_HEADS, HEAD_DIM))

        r2, e2, cnt, c1 = _score(pq_p, pq_s, sk1_bf, sk2_bf)
        f = _experts(hb_p, hb_s, pu_bf, pvt_bf, r2, e2, cnt, c1, TOKEN_TILE, EXPERT_TILE)
        xp = _final_norm(h_p, f, 0, g2, b2, alpha, TOKEN_TILE)
        xs = _final_norm(h_s, f, tp, g2, b2, alpha, TOKEN_TILE)

    st = lambda name: jnp.stack(outs[name])
    return (xp.reshape(bp, seq, D_MODEL), xs.reshape(bs, dec, D_MODEL),
            st("pool_p"), st("sbk_p"), st("sbv_p"), st("mk_p"), st("mv_p"),
            st("pool_s"), st("sbk_s"), st("sbv_s"))
```

```python
import functools
import math

import jax
import jax.numpy as jnp
from jax import lax
from jax.experimental import pallas as pl
from jax.experimental.pallas import tpu as pltpu

_BF = jnp.bfloat16
_F32 = jnp.float32

D_MODEL = 2048
HEAD_DIM = 128
LANES = 128
POOL_WIDTH = D_MODEL // 4
POOL_WINDOWS = (2, 4, 8, 16)
POOL_GROUP_DIM = POOL_WIDTH // len(POOL_WINDOWS)
POOL_HIST = max(POOL_WINDOWS) - 1
POOL_HALO = POOL_HIST + 1
SB_WIDTH = D_MODEL // 2
SB_HEADS = SB_WIDTH // HEAD_DIM
MEM_WIDTH = D_MODEL // 4
MEM_HEADS = MEM_WIDTH // HEAD_DIM
IN_WIDTH = POOL_WIDTH + 3 * SB_WIDTH + MEM_WIDTH
PEER_HEADS = 8
N_KEYS = 128
N_EXPERTS = N_KEYS * N_KEYS
PEER_QDIM = 256
PEER_HALF = PEER_QDIM // 2
PEER_TOPK = 16
LN_EPS = 1e-5
ATT_SCALE = 1.0 / math.sqrt(HEAD_DIM)
SB_ZERO_LOG = -104.0
NOT_RANKED = 99.0

VMEM_LIMIT = 56 * 1024 * 1024

ROW_TILE = 256
TOKEN_TILE = 512
EXPERT_TILE = 512
SB_TILE = 256
SB_HEADS_PER_STEP = 2
SCORE_TILE = 512


def _params(*sem):
    return pltpu.CompilerParams(dimension_semantics=sem, vmem_limit_bytes=VMEM_LIMIT)


def _nt_dot(a, b):
    return lax.dot_general(a, b, (((1,), (1,)), ((), ())), preferred_element_type=_F32)


def _dot(a, b):
    return jnp.dot(a, b, preferred_element_type=_F32)


def _resident(shape):
    nd = len(shape)
    return pl.BlockSpec(shape, lambda *_: (0,) * nd, pipeline_mode=pl.Buffered(1))


def _proj_body(x_ref, w_ref, *o_refs, plan):
    xb = x_ref[...].astype(_BF)
    n = 0
    for a, b, dtypes in plan:
        y = _dot(xb, w_ref[:, a:b])
        for dt in dtypes:
            o_refs[n][...] = y.astype(dt)
            n += 1


def _project(x, w_bf, plan, tm):
    t, k = x.shape
    n = w_bf.shape[1]
    shapes, specs = [], []
    for a, b, dtypes in plan:
        for dt in dtypes:
            shapes.append(jax.ShapeDtypeStruct((t, b - a), dt))
            specs.append(pl.BlockSpec((tm, b - a), lambda i: (i, 0)))
    return pl.pallas_call(
        functools.partial(_proj_body, plan=plan),
        out_shape=shapes,
        grid=(t // tm,),
        in_specs=[pl.BlockSpec((tm, k), lambda i: (i, 0)), _resident((k, n))],
        out_specs=specs,
        compiler_params=_params("parallel"),
        name="project",
    )(x, w_bf)


def _pool_body(p_ref, prev_ref, hist_ref, wp_ref, sc_ref, o_ref, buf_ref, *, tt, pos0):
    i = pl.program_id(1)
    buf_ref[0:POOL_HALO, :] = jnp.where(i == 0, hist_ref[0], prev_ref[0])
    buf_ref[POOL_HALO:POOL_HALO + tt, :] = p_ref[0]
    pos = pos0 + i * tt + lax.broadcasted_iota(jnp.int32, (tt, POOL_GROUP_DIM), 0)
    for g, w in enumerate(POOL_WINDOWS):
        cols = slice(g * POOL_GROUP_DIM, (g + 1) * POOL_GROUP_DIM)
        win = buf_ref[POOL_HALO - (w - 1):POOL_HALO - (w - 1) + tt, cols]
        for back in range(w - 2, -1, -1):
            win = win + buf_ref[POOL_HALO - back:POOL_HALO - back + tt, cols]
        cnt = jnp.minimum(pos + 1, w).astype(_F32)
        pooled = win / cnt - p_ref[0, :, cols]
        mixed = _dot(pooled.astype(_BF), wp_ref[g]) * sc_ref[:, cols]
        o_ref[0, :, cols] = mixed.astype(o_ref.dtype)


def _pool_mix(p, hist, wp_bf, scale, pos0, tt):
    b, t, c = p.shape
    hist16 = jnp.pad(hist, ((0, 0), (POOL_HALO - POOL_HIST, 0), (0, 0)))
    per = tt // POOL_HALO
    return pl.pallas_call(
        functools.partial(_pool_body, tt=tt, pos0=pos0),
        out_shape=jax.ShapeDtypeStruct((b, t, c), _BF),
        grid=(b, t // tt),
        in_specs=[
            pl.BlockSpec((1, tt, c), lambda bi, i: (bi, i, 0)),
            pl.BlockSpec((1, POOL_HALO, c), lambda bi, i: (bi, jnp.maximum(i * per - 1, 0), 0)),
            pl.BlockSpec((1, POOL_HALO, c), lambda bi, i: (bi, 0, 0)),
            _resident(wp_bf.shape),
            _resident(scale.shape),
        ],
        out_specs=pl.BlockSpec((1, tt, c), lambda bi, i: (bi, i, 0)),
        scratch_shapes=[pltpu.VMEM((POOL_HALO + tt, c), _F32)],
        compiler_params=_params("parallel", "parallel"),
        name="pool_mix",
    )(p, p, hist16, wp_bf, scale)


def _tri_ext(tk):
    j = lax.broadcasted_iota(jnp.int32, (tk, tk + LANES), 0)
    s = lax.broadcasted_iota(jnp.int32, (tk, tk + LANES), 1)
    return jnp.where((j > s) | (s >= tk), 1.0, 0.0).astype(_BF)


def _sb_block(q, kblk, vblk, c_b, tri, mask):
    tk = kblk.shape[0]
    z = _nt_dot(q, kblk) * ATT_SCALE
    softplus = jnp.maximum(z, 0.0) + jnp.log1p(jnp.exp(-jnp.abs(z)))
    log_keep = -softplus
    if mask is not None:
        log_keep = jnp.where(mask, log_keep, 0.0)
    hi = log_keep.astype(_BF)
    lo = (log_keep - hi.astype(_F32)).astype(_BF)
    ext = _dot(hi, tri) + _dot(lo, tri)
    between, total = ext[:, :tk], ext[:, tk:]
    newer = c_b if tk == LANES else jnp.concatenate([c_b] * (tk // LANES), axis=1)
    a = jnp.exp((z - softplus) + between + newer)
    if mask is not None:
        a = jnp.where(mask, a, 0.0)
    return _dot(a.astype(_BF), vblk), c_b + total


def _causal_mask(tq, tk):
    row = lax.broadcasted_iota(jnp.int32, (tq, tk), 0)
    col = lax.broadcasted_iota(jnp.int32, (tq, tk), 1)
    return col < row


def _sb_prompt_body(q_ref, k_ref, v_ref, tri_ref, o_ref, acc_ref, c_ref, *, tq, nh):
    i = pl.program_id(1)
    tri = tri_ref[...]
    heads = [slice(h * HEAD_DIM, (h + 1) * HEAD_DIM) for h in range(nh)]

    def step(j, mask, first):
        rows = pl.ds(pl.multiple_of(j * tq, tq), tq)
        top = None
        for h, cols in enumerate(heads):
            c_b = jnp.zeros((tq, LANES), _F32) if first else c_ref[h]
            pv, c = _sb_block(q_ref[:, cols], k_ref[rows, cols], v_ref[rows, cols], c_b, tri, mask)
            acc_ref[h] = pv if first else acc_ref[h] + pv
            c_ref[h] = c
            top = jnp.max(c) if top is None else jnp.maximum(top, jnp.max(c))
        return top < SB_ZERO_LOG

    def cond(state):
        j, done = state
        return jnp.logical_and(j >= 0, jnp.logical_not(done))

    def body(state):
        j, _ = state
        return j - 1, step(j, None, False)

    lax.while_loop(cond, body, (i - 1, step(i, _causal_mask(tq, tq), True)))
    for h, cols in enumerate(heads):
        o_ref[:, cols] = acc_ref[h].astype(o_ref.dtype)


def _sb_prompt(q_bf, k_bf, v_bf, tq, nh):
    t = q_bf.shape[0]
    tri = _tri_ext(tq)
    w = nh * HEAD_DIM
    return pl.pallas_call(
        functools.partial(_sb_prompt_body, tq=tq, nh=nh),
        out_shape=jax.ShapeDtypeStruct((t, SB_WIDTH), _BF),
        grid=(SB_HEADS // nh, t // tq),
        in_specs=[
            pl.BlockSpec((tq, w), lambda g, i: (i, g)),
            pl.BlockSpec((t, w), lambda g, i: (0, g)),
            pl.BlockSpec((t, w), lambda g, i: (0, g)),
            _resident(tri.shape),
        ],
        out_specs=pl.BlockSpec((tq, w), lambda g, i: (i, g)),
        scratch_shapes=[pltpu.VMEM((nh, tq, HEAD_DIM), _F32), pltpu.VMEM((nh, tq, LANES), _F32)],
        compiler_params=_params("parallel", "parallel"),
        name="sb_prompt",
    )(q_bf, k_bf, v_bf, tri)


def _sb_cached_body(q_ref, kn_ref, vn_ref, ck_ref, cv_ref, trin_ref, tric_ref, o_ref,
                    kbuf, vbuf, sem, acc_ref, c_ref, *, td, tn, tc, n_blocks):
    b = pl.program_id(0)
    mask = _causal_mask(td, tn)
    trin = trin_ref[...]
    for h in range(SB_HEADS):
        cols = slice(h * HEAD_DIM, (h + 1) * HEAD_DIM)
        pv, c = _sb_block(q_ref[:, cols], kn_ref[0, :, cols], vn_ref[0, :, cols],
                          jnp.zeros((td, LANES), _F32), trin, mask)
        acc_ref[h] = pv
        c_ref[h] = c

    def copies(j):
        rows = pl.ds(pl.multiple_of(j * tc, tc), tc)
        return (pltpu.make_async_copy(ck_ref.at[b, rows], kbuf, sem.at[0]),
                pltpu.make_async_copy(cv_ref.at[b, rows], vbuf, sem.at[1]))

    def cond(state):
        j, done = state
        return jnp.logical_and(j >= 0, jnp.logical_not(done))

    def body(state):
        j, _ = state
        ck, cv = copies(j)
        ck.start()
        cv.start()
        ck.wait()
        cv.wait()
        tric = tric_ref[...]
        top = jnp.float32(-jnp.inf)
        for h in range(SB_HEADS):
            cols = slice(h * HEAD_DIM, (h + 1) * HEAD_DIM)
            pv, c = _sb_block(q_ref[:, cols], kbuf[:, h, :].astype(_BF), vbuf[:, h, :].astype(_BF),
                              c_ref[h], tric, None)
            acc_ref[h] += pv
            c_ref[h] = c
            top = jnp.maximum(top, jnp.max(c))
        return j - 1, top < SB_ZERO_LOG

    lax.while_loop(cond, body, (jnp.int32(n_blocks - 1), jnp.max(c_ref[...]) < SB_ZERO_LOG))
    for h in range(SB_HEADS):
        o_ref[:, h * HEAD_DIM:(h + 1) * HEAD_DIM] = acc_ref[h].astype(o_ref.dtype)


def _sb_cached(q_bf, k_new, v_new, cache_k, cache_v, td, tc):
    nb, past, heads, hd = cache_k.shape
    width = heads * hd
    tn = LANES
    pad = ((0, 0), (0, tn - td), (0, 0))
    kn, vn = jnp.pad(k_new, pad), jnp.pad(v_new, pad)
    trin, tric = _tri_ext(tn), _tri_ext(tc)
    return pl.pallas_call(
        functools.partial(_sb_cached_body, td=td, tn=tn, tc=tc, n_blocks=past // tc),
        out_shape=jax.ShapeDtypeStruct((nb * td, width), _BF),
        grid=(nb,),
        in_specs=[
            pl.BlockSpec((td, width), lambda b: (b, 0)),
            pl.BlockSpec((1, tn, width), lambda b: (b, 0, 0)),
            pl.BlockSpec((1, tn, width), lambda b: (b, 0, 0)),
            pl.BlockSpec(memory_space=pl.ANY),
            pl.BlockSpec(memory_space=pl.ANY),
            _resident(trin.shape),
            _resident(tric.shape),
        ],
        out_specs=pl.BlockSpec((td, width), lambda b: (b, 0)),
        scratch_shapes=[
            pltpu.VMEM((tc, heads, hd), _F32),
            pltpu.VMEM((tc, heads, hd), _F32),
            pltpu.SemaphoreType.DMA((2,)),
            pltpu.VMEM((SB_HEADS, td, HEAD_DIM), _F32),
            pltpu.VMEM((SB_HEADS, td, LANES), _F32),
        ],
        compiler_params=_params("arbitrary"),
        name="sb_cached",
    )(q_bf, kn, vn, cache_k, cache_v, trin, tric)


def _mem_body(q_ref, k_ref, v_ref, o_ref):
    for h in range(MEM_HEADS):
        cols = slice(h * HEAD_DIM, (h + 1) * HEAD_DIM)
        s = _nt_dot(q_ref[0, :, cols], k_ref[0, :, h, :].astype(_BF)) * ATT_SCALE
        e = jnp.exp(s - jnp.max(s, axis=1, keepdims=True))
        pr = e / jnp.sum(e, axis=1, keepdims=True)
        o_ref[0, :, cols] = _dot(pr.astype(_BF), v_ref[0, :, h, :].astype(_BF)).astype(o_ref.dtype)


def _mem_attend(qm_bf, mk, mv, tt):
    b, t, w = qm_bf.shape
    _, m, heads, hd = mk.shape
    return pl.pallas_call(
        _mem_body,
        out_shape=jax.ShapeDtypeStruct((b, t, w), _BF),
        grid=(b, t // tt),
        in_specs=[
            pl.BlockSpec((1, tt, w), lambda bi, i: (bi, i, 0)),
            pl.BlockSpec((1, m, heads, hd), lambda bi, i: (bi, 0, 0, 0)),
            pl.BlockSpec((1, m, heads, hd), lambda bi, i: (bi, 0, 0, 0)),
        ],
        out_specs=pl.BlockSpec((1, tt, w), lambda bi, i: (bi, i, 0)),
        compiler_params=_params("parallel", "parallel"),
        name="mem_attend",
    )(qm_bf, mk, mv)


def _layer_norm(r, g, b):
    mu = jnp.mean(r, axis=-1, keepdims=True)
    d = r - mu
    var = jnp.mean(d * d, axis=-1, keepdims=True)
    return d * lax.rsqrt(var + LN_EPS) * g + b


def _tail_body(x_ref, pool_ref, sb_ref, mem_ref, wo_ref, g_ref, b_ref, wq_ref, h_ref, hb_ref, pq_ref, *, alpha):
    o1, o2 = POOL_WIDTH, POOL_WIDTH + SB_WIDTH
    mix = (_dot(pool_ref[...], wo_ref[0:o1, :]) + _dot(sb_ref[...], wo_ref[o1:o2, :])
           + _dot(mem_ref[...], wo_ref[o2:, :]))
    h = _layer_norm(alpha * x_ref[...] + mix, g_ref[...], b_ref[...])
    h_ref[...] = h
    hb = h.astype(_BF)
    hb_ref[...] = hb
    for hd in range(PEER_HEADS):
        pq_ref[hd] = _dot(hb, wq_ref[:, hd * PEER_QDIM:(hd + 1) * PEER_QDIM]).astype(_BF)


def _tail(x, pool_o, sb_o, mem_o, wo_bf, g, b, wq_bf, alpha, tm):
    t = x.shape[0]
    row = lambda width: pl.BlockSpec((tm, width), lambda i: (i, 0))
    return pl.pallas_call(
        functools.partial(_tail_body, alpha=alpha),
        out_shape=[
            jax.ShapeDtypeStruct((t, D_MODEL), _F32),
            jax.ShapeDtypeStruct((t, D_MODEL), _BF),
            jax.ShapeDtypeStruct((PEER_HEADS, t, PEER_QDIM), _BF),
        ],
        grid=(t // tm,),
        in_specs=[row(D_MODEL), row(POOL_WIDTH), row(SB_WIDTH), row(MEM_WIDTH),
                  _resident(wo_bf.shape), _resident(g.shape), _resident(b.shape), _resident(wq_bf.shape)],
        out_specs=[row(D_MODEL), row(D_MODEL),
                   pl.BlockSpec((PEER_HEADS, tm, PEER_QDIM), lambda i: (0, i, 0))],
        compiler_params=_params("parallel"),
        name="tail",
    )(x, pool_o, sb_o, mem_o, wo_bf, g, b, wq_bf)


def _top_ranked(x, k):
    rows = lax.broadcasted_iota(jnp.int32, x.shape, 0).astype(_F32)
    work = x
    rank = jnp.full(x.shape, NOT_RANKED, _F32)
    vals = []
    for r in range(k):
        m = jnp.max(work, axis=0, keepdims=True)
        first = jnp.min(jnp.where(work == m, rows, float(x.shape[0])), axis=0, keepdims=True)
        sel = rows == first
        rank = jnp.where(sel, float(r), rank)
        work = jnp.where(sel, -jnp.inf, work)
        vals.append(m)
    return rank, jnp.concatenate(vals, axis=0)


def _top_ranked_untied(x, k):
    work = x
    rank = jnp.full(x.shape, NOT_RANKED, _F32)
    vals = []
    for r in range(k):
        m = jnp.max(work, axis=0, keepdims=True)
        sel = work == m
        rank = jnp.where(sel, float(r), rank)
        work = jnp.where(sel, -jnp.inf, work)
        vals.append(m)
    return rank, jnp.concatenate(vals, axis=0)


def _all_ranked_once(rank, k):
    n = jnp.sum(jnp.where(rank < NOT_RANKED, 1.0, 0.0), axis=0, keepdims=True)
    return jnp.max(jnp.abs(n - float(k))) == 0.0


def _score_body(pqp_ref, pqs_ref, sk1_ref, sk2_ref, r2_ref, e2_ref, cnt_ref, c1_ref,
                rank1_s, t1_s, rank2_s, t2_s, *, prompt_steps):
    k = PEER_TOPK
    tw = r2_ref.shape[2]
    arow = lax.broadcasted_iota(jnp.int32, (k, tw), 0).astype(_F32)
    from_prompt = pl.program_id(0) < prompt_steps

    def head(h, carry):
        pq = jnp.where(from_prompt, pqp_ref[h], pqs_ref[h])
        s1 = _nt_dot(sk1_ref[h], pq[:, :PEER_HALF])
        s2 = _nt_dot(sk2_ref[h], pq[:, PEER_HALF:])
        rank1_s[...], t1_s[...] = _top_ranked_untied(s1, k)
        rank2_s[...], t2_s[...] = _top_ranked_untied(s2, k)
        untied = jnp.logical_and(_all_ranked_once(rank1_s[...], k), _all_ranked_once(rank2_s[...], k))

        @pl.when(jnp.logical_not(untied))
        def _():
            rank1_s[...], t1_s[...] = _top_ranked(s1, k)
            rank2_s[...], t2_s[...] = _top_ranked(s2, k)

        rank1, t1, rank2, t2 = rank1_s[...], t1_s[...], rank2_s[...], t2_s[...]
        taken = jnp.zeros((k, tw), _F32)
        best0 = t1[0:1] + t2[0:1]
        z = jnp.zeros((1, tw), _F32)
        for _ in range(k):
            nxt = jnp.full((k, tw), -jnp.inf, _F32)
            for bcol in range(k):
                nxt = jnp.where(taken == float(bcol), t2[bcol:bcol + 1], nxt)
            front = t1 + nxt
            m = jnp.max(front, axis=0, keepdims=True)
            first = jnp.min(jnp.where(front == m, arow, float(k)), axis=0, keepdims=True)
            taken = taken + jnp.where(arow == first, 1.0, 0.0)
            z = z + jnp.exp(m - best0)
        cnt = jnp.zeros((N_KEYS, tw), _F32)
        for a in range(k):
            cnt = jnp.where(rank1 == float(a), taken[a:a + 1], cnt)
        r2_ref[h] = rank2.astype(r2_ref.dtype)
        e2_ref[h] = jnp.exp(s2 - t2[0:1]).astype(e2_ref.dtype)
        cnt_ref[h] = cnt
        c1_ref[h] = jnp.exp(s1 - t1[0:1]) / z
        return carry

    lax.fori_loop(0, PEER_HEADS, head, 0)


def _score(pq_p, pq_s, sk1_bf, sk2_bf):
    tp, ts = pq_p.shape[1], pq_s.shape[1]
    t = tp + ts
    tw = SCORE_TILE
    np_ = tp // tw
    out = lambda dt: jax.ShapeDtypeStruct((PEER_HEADS, N_KEYS, t), dt)
    spec = pl.BlockSpec((PEER_HEADS, N_KEYS, tw), lambda i: (0, 0, i))
    return pl.pallas_call(
        functools.partial(_score_body, prompt_steps=np_),
        out_shape=[out(_BF), out(_BF), out(_F32), out(_F32)],
        grid=(t // tw,),
        in_specs=[pl.BlockSpec((PEER_HEADS, tw, PEER_QDIM), lambda i: (0, jnp.minimum(i, np_ - 1), 0)),
                  pl.BlockSpec((PEER_HEADS, tw, PEER_QDIM), lambda i: (0, jnp.maximum(i - np_, 0), 0)),
                  _resident(sk1_bf.shape), _resident(sk2_bf.shape)],
        out_specs=[spec] * 4,
        scratch_shapes=[pltpu.VMEM((N_KEYS, tw), _F32), pltpu.VMEM((PEER_TOPK, tw), _F32),
                        pltpu.VMEM((N_KEYS, tw), _F32), pltpu.VMEM((PEER_TOPK, tw), _F32)],
        compiler_params=_params("parallel"),
        name="peer_score",
    )(pq_p, pq_s, sk1_bf, sk2_bf)


def _gelu(a):
    return 0.5 * a * (1.0 + lax.erf(a * (1.0 / math.sqrt(2.0))))


def _activation_tile(pu_rows, hb_ref, act_ref):
    act_ref[...] = _gelu(_nt_dot(pu_rows, hb_ref[...])).astype(_BF)


def _gated_tile(act_ref, r2_ref, e2_ref, cnt_ref, c1_ref, tile):
    te, tm = act_ref.shape
    zero = jnp.zeros((), _BF)
    out = []
    for s in range(te // N_KEYS):
        i1 = tile * (te // N_KEYS) + s
        gate = jnp.zeros((N_KEYS, tm), _BF)
        for h in range(PEER_HEADS):
            thr = cnt_ref[h, pl.ds(i1, 1), :].astype(_BF)
            c1 = c1_ref[h, pl.ds(i1, 1), :].astype(_BF)
            gate = gate + jnp.where(r2_ref[h] < thr, e2_ref[h] * c1, zero)
        out.append(gate * act_ref[s * N_KEYS:(s + 1) * N_KEYS, :])
    return jnp.concatenate(out, axis=0)


def _expert_body(hbp_ref, hbs_ref, pu_ref, pvp_ref, pvc_ref, r2_ref, e2_ref, cnt_ref, c1_ref, o_ref,
                 hb_ref, acc_ref, acta_ref, actb_ref, *, te, pairs, prompt_tiles):
    i = pl.program_id(0)
    jj = pl.program_id(1)
    aux = (r2_ref, e2_ref, cnt_ref, c1_ref)

    @pl.when(jnp.logical_and(jj == 0, i < prompt_tiles))
    def _():
        hb_ref[...] = hbp_ref[...]

    @pl.when(jnp.logical_and(jj == 0, i >= prompt_tiles))
    def _():
        hb_ref[...] = hbs_ref[...]

    @pl.when(jj == 0)
    def _():
        _activation_tile(pu_ref[0:te, :], hb_ref, acta_ref)
        _activation_tile(pu_ref[te:2 * te, :], hb_ref, actb_ref)
        acc_ref[...] = _dot(pvc_ref[...], _gated_tile(acta_ref, *aux, 0))

    @pl.when(jnp.logical_and(jj > 0, jj < pairs))
    def _():
        w_prev = _gated_tile(actb_ref, *aux, 2 * jj - 1)
        _activation_tile(pu_ref[0:te, :], hb_ref, acta_ref)
        acc = acc_ref[...] + _dot(pvp_ref[...], w_prev)
        w_cur = _gated_tile(acta_ref, *aux, 2 * jj)
        _activation_tile(pu_ref[te:2 * te, :], hb_ref, actb_ref)
        acc_ref[...] = acc + _dot(pvc_ref[...], w_cur)

    @pl.when(jj == pairs)
    def _():
        w_prev = _gated_tile(actb_ref, *aux, 2 * pairs - 1)
        o_ref[...] = (acc_ref[...] + _dot(pvp_ref[...], w_prev)).T


def _experts(hb_p, hb_s, pu_bf, pvt_bf, r2, e2, cnt, c1, tm, te):
    tp, ts = hb_p.shape[0], hb_s.shape[0]
    t = tp + ts
    npt = tp // tm
    tiles = N_EXPERTS // te
    pairs = tiles // 2
    aux = pl.BlockSpec((PEER_HEADS, N_KEYS, tm), lambda i, j: (0, 0, i), pipeline_mode=pl.Buffered(1))
    return pl.pallas_call(
        functools.partial(_expert_body, te=te, pairs=pairs, prompt_tiles=npt),
        out_shape=jax.ShapeDtypeStruct((t, D_MODEL), _F32),
        grid=(t // tm, pairs + 1),
        in_specs=[
            pl.BlockSpec((tm, D_MODEL), lambda i, j: (jnp.minimum(i, npt - 1), 0)),
            pl.BlockSpec((tm, D_MODEL), lambda i, j: (jnp.maximum(i - npt, 0), 0)),
            pl.BlockSpec((2 * te, D_MODEL), lambda i, j: (jnp.minimum(j, pairs - 1), 0)),
            pl.BlockSpec((None, D_MODEL, te), lambda i, j: (jnp.maximum(2 * j - 1, 0), 0, 0)),
            pl.BlockSpec((None, D_MODEL, te), lambda i, j: (jnp.minimum(2 * j, tiles - 1), 0, 0)),
            aux, aux, aux, aux,
        ],
        out_specs=pl.BlockSpec((tm, D_MODEL), lambda i, j: (i, 0)),
        scratch_shapes=[pltpu.VMEM((tm, D_MODEL), _BF), pltpu.VMEM((D_MODEL, tm), _F32),
                        pltpu.VMEM((te, tm), _BF), pltpu.VMEM((te, tm), _BF)],
        compiler_params=_params("parallel", "arbitrary"),
        name="peer_experts",
    )(hb_p, hb_s, pu_bf, pvt_bf, pvt_bf, r2, e2, cnt, c1)


def _final_body(h_ref, f_ref, g_ref, b_ref, o_ref, *, alpha):
    o_ref[...] = _layer_norm(alpha * h_ref[...] + f_ref[...], g_ref[...], b_ref[...])


def _final_norm(h, f_all, row0, g, b, alpha, tm):
    t = h.shape[0]
    off = row0 // tm
    return pl.pallas_call(
        functools.partial(_final_body, alpha=alpha),
        out_shape=jax.ShapeDtypeStruct((t, D_MODEL), _F32),
        grid=(t // tm,),
        in_specs=[pl.BlockSpec((tm, D_MODEL), lambda i: (i, 0)),
                  pl.BlockSpec((tm, D_MODEL), lambda i: (i + off, 0)),
                  _resident(g.shape), _resident(b.shape)],
        out_specs=pl.BlockSpec((tm, D_MODEL), lambda i: (i, 0)),
        compiler_params=_params("parallel"),
        name="final_norm",
    )(h, f_all, g, b)


_IN_PLAN = (
    (0, POOL_WIDTH, (_F32,)),
    (POOL_WIDTH, POOL_WIDTH + SB_WIDTH, (_BF,)),
    (POOL_WIDTH + SB_WIDTH, POOL_WIDTH + 2 * SB_WIDTH, (_F32, _BF)),
    (POOL_WIDTH + 2 * SB_WIDTH, POOL_WIDTH + 3 * SB_WIDTH, (_F32, _BF)),
    (POOL_WIDTH + 3 * SB_WIDTH, IN_WIDTH, (_BF,)),
)
_MEM_PLAN = ((0, MEM_WIDTH, (_F32,)), (MEM_WIDTH, 2 * MEM_WIDTH, (_F32,)))


def kernel(x_prompt, x_sample, state_pool, cache_sb_k, cache_sb_v, cache_mem_k, cache_mem_v, mem_prompt, w_in, w_pool, pool_scale, w_mem_kv, w_out, ln1_g, ln1_b, peer_wq, peer_subkey1, peer_subkey2, peer_u, peer_v, ln2_g, ln2_b):
    depth = w_in.shape[0]
    bp, seq, _ = x_prompt.shape
    bs, dec, _ = x_sample.shape
    past = cache_sb_k.shape[2]
    n_mem = mem_prompt.shape[1]
    assert bp == 1, "the prompt group is one stream"
    alpha = (2.0 * depth) ** 0.25
    tp, ts = seq * bp, bs * dec

    xp = x_prompt.reshape(tp, D_MODEL)
    xs = x_sample.reshape(ts, D_MODEL)
    outs = {k: [] for k in ("pool_p", "sbk_p", "sbv_p", "mk_p", "mv_p", "pool_s", "sbk_s", "sbv_s")}
    def layer(a, l):
        return a.reshape(a.shape[1:]) if a.shape[0] == 1 else a[l]

    for l in range(depth):
        w_in_bf = layer(w_in, l).astype(_BF)
        wp_bf = layer(w_pool, l).astype(_BF)
        scale = layer(pool_scale, l).reshape(1, POOL_WIDTH)
        wo_bf = layer(w_out, l).astype(_BF)
        wq_bf = layer(peer_wq, l).astype(_BF)
        sk1_bf = layer(peer_subkey1, l).astype(_BF)
        sk2_bf = layer(peer_subkey2, l).astype(_BF)
        pu_bf = layer(peer_u, l).astype(_BF)
        pvt_bf = jnp.swapaxes(layer(peer_v, l).reshape(N_EXPERTS // EXPERT_TILE, EXPERT_TILE, D_MODEL),
                              1, 2).astype(_BF)
        g1, b1 = layer(ln1_g, l).reshape(1, D_MODEL), layer(ln1_b, l).reshape(1, D_MODEL)
        g2, b2 = layer(ln2_g, l).reshape(1, D_MODEL), layer(ln2_b, l).reshape(1, D_MODEL)
        hist_s0 = layer(state_pool, l)

        p, q, k, kb, v, vb, qm = _project(xp, w_in_bf, _IN_PLAN, ROW_TILE)
        p3 = p.reshape(bp, seq, POOL_WIDTH)
        pool_o = _pool_mix(p3, jnp.zeros((bp, POOL_HIST, POOL_WIDTH), _F32), wp_bf, scale, 0, TOKEN_TILE)
        sb_o = _sb_prompt(q, kb, vb, SB_TILE, SB_HEADS_PER_STEP)
        mk, mv = _project(mem_prompt.reshape(bp * n_mem, D_MODEL), layer(w_mem_kv, l).astype(_BF), _MEM_PLAN,
                          ROW_TILE)
        mk = mk.reshape(bp, n_mem, MEM_HEADS, HEAD_DIM)
        mv = mv.reshape(bp, n_mem, MEM_HEADS, HEAD_DIM)
        mem_o = _mem_attend(qm.reshape(bp, seq, MEM_WIDTH), mk, mv, TOKEN_TILE)
        h_p, hb_p, pq_p = _tail(xp, pool_o.reshape(tp, POOL_WIDTH), sb_o, mem_o.reshape(tp, MEM_WIDTH),
                                wo_bf, g1, b1, wq_bf, alpha, ROW_TILE)
        outs["pool_p"].append(p3[:, seq - POOL_HIST:])
        outs["sbk_p"].append(k.reshape(bp, seq, SB_HEADS, HEAD_DIM))
        outs["sbv_p"].append(v.reshape(bp, seq, SB_HEADS, HEAD_DIM))
        outs["mk_p"].append(mk)
        outs["mv_p"].append(mv)

        p, q, k, kb, v, vb, qm = _project(xs, w_in_bf, _IN_PLAN, ROW_TILE)
        p3 = p.reshape(bs, dec, POOL_WIDTH)
        pool_o = _pool_mix(p3, hist_s0, wp_bf, scale, past, dec)
        sb_o = _sb_cached(q, kb.reshape(bs, dec, SB_WIDTH), vb.reshape(bs, dec, SB_WIDTH),
                          layer(cache_sb_k, l), layer(cache_sb_v, l), dec, SB_TILE)
        mem_o = _mem_attend(qm.reshape(bs, dec, MEM_WIDTH), layer(cache_mem_k, l), layer(cache_mem_v, l), dec)
        h_s, hb_s, pq_s = _tail(xs, pool_o.reshape(ts, POOL_WIDTH), sb_o, mem_o.reshape(ts, MEM_WIDTH),
                                wo_bf, g1, b1, wq_bf, alpha, ROW_TILE)
        hist_s = jnp.concatenate([hist_s0, p3], axis=1)[:, -POOL_HIST:]
        outs["pool_s"].append(hist_s)
        outs["sbk_s"].append(k.reshape(bs, dec, SB_HEADS, HEAD_DIM))
        outs["sbv_s"].append(v.reshape(bs, dec, SB_HEADS, HEAD_DIM))

        r2, e2, cnt, c1 = _score(pq_p, pq_s, sk1_bf, sk2_bf)
        f = _experts(hb_p, hb_s, pu_bf, pvt_bf, r2, e2, cnt, c1, TOKEN_TILE, EXPERT_TILE)
        xp = _final_norm(h_p, f, 0, g2, b2, alpha, TOKEN_TILE)
        xs = _final_norm(h_s, f, tp, g2, b2, alpha, TOKEN_TILE)

    st = lambda name: jnp.stack(outs[name])
    return (xp.reshape(bp, seq, D_MODEL), xs.reshape(bs, dec, D_MODEL),
            st("pool_p"), st("sbk_p"), st("sbv_p"), st("mk_p"), st("mv_p"),
            st("pool_s"), st("sbk_s"), st("sbv_s"))
```

```python
import functools
import math

import jax
import jax.numpy as jnp
from jax import lax
from jax.experimental import pallas as pl
from jax.experimental.pallas import tpu as pltpu

_BF = jnp.bfloat16
_F32 = jnp.float32

D_MODEL = 2048
HEAD_DIM = 128
LANES = 128
POOL_WIDTH = D_MODEL // 4
POOL_WINDOWS = (2, 4, 8, 16)
POOL_GROUP_DIM = POOL_WIDTH // len(POOL_WINDOWS)
POOL_HIST = max(POOL_WINDOWS) - 1
POOL_HALO = POOL_HIST + 1
SB_WIDTH = D_MODEL // 2
SB_HEADS = SB_WIDTH // HEAD_DIM
MEM_WIDTH = D_MODEL // 4
MEM_HEADS = MEM_WIDTH // HEAD_DIM
IN_WIDTH = POOL_WIDTH + 3 * SB_WIDTH + MEM_WIDTH
PEER_HEADS = 8
N_KEYS = 128
N_EXPERTS = N_KEYS * N_KEYS
PEER_QDIM = 256
PEER_HALF = PEER_QDIM // 2
PEER_TOPK = 16
LN_EPS = 1e-5
ATT_SCALE = 1.0 / math.sqrt(HEAD_DIM)
SB_ZERO_LOG = -104.0
NOT_RANKED = 99.0

VMEM_LIMIT = 56 * 1024 * 1024

ROW_TILE = 256
TOKEN_TILE = 512
EXPERT_TILE = 512
EXPERT_TOKEN_TILE = 1024
SB_TILE = 256
SB_HEADS_PER_STEP = 4
SCORE_TILE = 512


def _params(*sem):
    return pltpu.CompilerParams(dimension_semantics=sem, vmem_limit_bytes=VMEM_LIMIT)


def _nt_dot(a, b):
    return lax.dot_general(a, b, (((1,), (1,)), ((), ())), preferred_element_type=_F32)


def _dot(a, b):
    return jnp.dot(a, b, preferred_element_type=_F32)


def _resident(shape):
    nd = len(shape)
    return pl.BlockSpec(shape, lambda *_: (0,) * nd, pipeline_mode=pl.Buffered(1))


def _proj_body(x_ref, w_ref, *o_refs, plan):
    xb = x_ref[...].astype(_BF)
    n = 0
    for a, b, dtypes in plan:
        y = _dot(xb, w_ref[:, a:b])
        for dt in dtypes:
            o_refs[n][...] = y.astype(dt)
            n += 1


def _project(x, w_bf, plan, tm):
    t, k = x.shape
    n = w_bf.shape[1]
    shapes, specs = [], []
    for a, b, dtypes in plan:
        for dt in dtypes:
            shapes.append(jax.ShapeDtypeStruct((t, b - a), dt))
            specs.append(pl.BlockSpec((tm, b - a), lambda i: (i, 0)))
    return pl.pallas_call(
        functools.partial(_proj_body, plan=plan),
        out_shape=shapes,
        grid=(t // tm,),
        in_specs=[pl.BlockSpec((tm, k), lambda i: (i, 0)), _resident((k, n))],
        out_specs=specs,
        compiler_params=_params("parallel"),
        name="project",
    )(x, w_bf)


def _pool_body(p_ref, prev_ref, hist_ref, wp_ref, sc_ref, o_ref, buf_ref, *, tt, pos0):
    i = pl.program_id(1)
    buf_ref[0:POOL_HALO, :] = jnp.where(i == 0, hist_ref[0], prev_ref[0])
    buf_ref[POOL_HALO:POOL_HALO + tt, :] = p_ref[0]
    pos = pos0 + i * tt + lax.broadcasted_iota(jnp.int32, (tt, POOL_GROUP_DIM), 0)
    for g, w in enumerate(POOL_WINDOWS):
        cols = slice(g * POOL_GROUP_DIM, (g + 1) * POOL_GROUP_DIM)
        win = buf_ref[POOL_HALO - (w - 1):POOL_HALO - (w - 1) + tt, cols]
        for back in range(w - 2, -1, -1):
            win = win + buf_ref[POOL_HALO - back:POOL_HALO - back + tt, cols]
        cnt = jnp.minimum(pos + 1, w).astype(_F32)
        pooled = win / cnt - p_ref[0, :, cols]
        mixed = _dot(pooled.astype(_BF), wp_ref[g]) * sc_ref[:, cols]
        o_ref[0, :, cols] = mixed.astype(o_ref.dtype)


def _pool_mix(p, hist, wp_bf, scale, pos0, tt):
    b, t, c = p.shape
    hist16 = jnp.pad(hist, ((0, 0), (POOL_HALO - POOL_HIST, 0), (0, 0)))
    per = tt // POOL_HALO
    return pl.pallas_call(
        functools.partial(_pool_body, tt=tt, pos0=pos0),
        out_shape=jax.ShapeDtypeStruct((b, t, c), _BF),
        grid=(b, t // tt),
        in_specs=[
            pl.BlockSpec((1, tt, c), lambda bi, i: (bi, i, 0)),
            pl.BlockSpec((1, POOL_HALO, c), lambda bi, i: (bi, jnp.maximum(i * per - 1, 0), 0)),
            pl.BlockSpec((1, POOL_HALO, c), lambda bi, i: (bi, 0, 0)),
            _resident(wp_bf.shape),
            _resident(scale.shape),
        ],
        out_specs=pl.BlockSpec((1, tt, c), lambda bi, i: (bi, i, 0)),
        scratch_shapes=[pltpu.VMEM((POOL_HALO + tt, c), _F32)],
        compiler_params=_params("parallel", "parallel"),
        name="pool_mix",
    )(p, p, hist16, wp_bf, scale)


def _tri_ext(tk):
    j = lax.broadcasted_iota(jnp.int32, (tk, tk + LANES), 0)
    s = lax.broadcasted_iota(jnp.int32, (tk, tk + LANES), 1)
    return jnp.where((j > s) | (s >= tk), 1.0, 0.0).astype(_BF)


def _sb_block(q, kblk, vblk, c_b, tri, mask):
    tk = kblk.shape[0]
    z = _nt_dot(q, kblk) * ATT_SCALE
    softplus = jnp.maximum(z, 0.0) + jnp.log1p(jnp.exp(-jnp.abs(z)))
    log_keep = -softplus
    if mask is not None:
        log_keep = jnp.where(mask, log_keep, 0.0)
    hi = log_keep.astype(_BF)
    lo = (log_keep - hi.astype(_F32)).astype(_BF)
    ext = _dot(hi, tri) + _dot(lo, tri)
    between, total = ext[:, :tk], ext[:, tk:]
    newer = c_b if tk == LANES else jnp.concatenate([c_b] * (tk // LANES), axis=1)
    a = jnp.exp((z - softplus) + between + newer)
    if mask is not None:
        a = jnp.where(mask, a, 0.0)
    return _dot(a.astype(_BF), vblk), c_b + total


def _causal_mask(tq, tk):
    row = lax.broadcasted_iota(jnp.int32, (tq, tk), 0)
    col = lax.broadcasted_iota(jnp.int32, (tq, tk), 1)
    return col < row


def _sb_prompt_body(q_ref, k_ref, v_ref, tri_ref, o_ref, acc_ref, c_ref, *, tq, nh):
    i = pl.program_id(1)
    tri = tri_ref[...]
    heads = [slice(h * HEAD_DIM, (h + 1) * HEAD_DIM) for h in range(nh)]

    def step(j, mask, first):
        rows = pl.ds(pl.multiple_of(j * tq, tq), tq)
        top = None
        for h, cols in enumerate(heads):
            c_b = jnp.zeros((tq, LANES), _F32) if first else c_ref[h]
            pv, c = _sb_block(q_ref[:, cols], k_ref[rows, cols], v_ref[rows, cols], c_b, tri, mask)
            acc_ref[h] = pv if first else acc_ref[h] + pv
            c_ref[h] = c
            top = jnp.max(c) if top is None else jnp.maximum(top, jnp.max(c))
        return top < SB_ZERO_LOG

    def cond(state):
        j, done = state
        return jnp.logical_and(j >= 0, jnp.logical_not(done))

    def body(state):
        j, _ = state
        return j - 1, step(j, None, False)

    lax.while_loop(cond, body, (i - 1, step(i, _causal_mask(tq, tq), True)))
    for h, cols in enumerate(heads):
        o_ref[:, cols] = acc_ref[h].astype(o_ref.dtype)


def _sb_prompt(q_bf, k_bf, v_bf, tq, nh):
    t = q_bf.shape[0]
    tri = _tri_ext(tq)
    w = nh * HEAD_DIM
    return pl.pallas_call(
        functools.partial(_sb_prompt_body, tq=tq, nh=nh),
        out_shape=jax.ShapeDtypeStruct((t, SB_WIDTH), _BF),
        grid=(SB_HEADS // nh, t // tq),
        in_specs=[
            pl.BlockSpec((tq, w), lambda g, i: (i, g)),
            pl.BlockSpec((t, w), lambda g, i: (0, g), pipeline_mode=pl.Buffered(1)),
            pl.BlockSpec((t, w), lambda g, i: (0, g), pipeline_mode=pl.Buffered(1)),
            _resident(tri.shape),
        ],
        out_specs=pl.BlockSpec((tq, w), lambda g, i: (i, g)),
        scratch_shapes=[pltpu.VMEM((nh, tq, HEAD_DIM), _F32), pltpu.VMEM((nh, tq, LANES), _F32)],
        compiler_params=_params("parallel", "parallel"),
        name="sb_prompt",
    )(q_bf, k_bf, v_bf, tri)


def _sb_cached_body(q_ref, kn_ref, vn_ref, ck_ref, cv_ref, trin_ref, tric_ref, o_ref,
                    kbuf, vbuf, sem, acc_ref, c_ref, *, td, tn, tc, n_blocks):
    b = pl.program_id(0)
    mask = _causal_mask(td, tn)
    trin = trin_ref[...]
    for h in range(SB_HEADS):
        cols = slice(h * HEAD_DIM, (h + 1) * HEAD_DIM)
        pv, c = _sb_block(q_ref[:, cols], kn_ref[0, :, cols], vn_ref[0, :, cols],
                          jnp.zeros((td, LANES), _F32), trin, mask)
        acc_ref[h] = pv
        c_ref[h] = c

    def copies(j):
        rows = pl.ds(pl.multiple_of(j * tc, tc), tc)
        return (pltpu.make_async_copy(ck_ref.at[b, rows], kbuf, sem.at[0]),
                pltpu.make_async_copy(cv_ref.at[b, rows], vbuf, sem.at[1]))

    def cond(state):
        j, done = state
        return jnp.logical_and(j >= 0, jnp.logical_not(done))

    def body(state):
        j, _ = state
        ck, cv = copies(j)
        ck.start()
        cv.start()
        ck.wait()
        cv.wait()
        tric = tric_ref[...]
        top = jnp.float32(-jnp.inf)
        for h in range(SB_HEADS):
            cols = slice(h * HEAD_DIM, (h + 1) * HEAD_DIM)
            pv, c = _sb_block(q_ref[:, cols], kbuf[:, h, :].astype(_BF), vbuf[:, h, :].astype(_BF),
                              c_ref[h], tric, None)
            acc_ref[h] += pv
            c_ref[h] = c
            top = jnp.maximum(top, jnp.max(c))
        return j - 1, top < SB_ZERO_LOG

    lax.while_loop(cond, body, (jnp.int32(n_blocks - 1), jnp.max(c_ref[...]) < SB_ZERO_LOG))
    for h in range(SB_HEADS):
        o_ref[:, h * HEAD_DIM:(h + 1) * HEAD_DIM] = acc_ref[h].astype(o_ref.dtype)


def _sb_cached(q_bf, k_new, v_new, cache_k, cache_v, td, tc):
    nb, past, heads, hd = cache_k.shape
    width = heads * hd
    tn = LANES
    pad = ((0, 0), (0, tn - td), (0, 0))
    kn, vn = jnp.pad(k_new, pad), jnp.pad(v_new, pad)
    trin, tric = _tri_ext(tn), _tri_ext(tc)
    return pl.pallas_call(
        functools.partial(_sb_cached_body, td=td, tn=tn, tc=tc, n_blocks=past // tc),
        out_shape=jax.ShapeDtypeStruct((nb * td, width), _BF),
        grid=(nb,),
        in_specs=[
            pl.BlockSpec((td, width), lambda b: (b, 0)),
            pl.BlockSpec((1, tn, width), lambda b: (b, 0, 0)),
            pl.BlockSpec((1, tn, width), lambda b: (b, 0, 0)),
            pl.BlockSpec(memory_space=pl.ANY),
            pl.BlockSpec(memory_space=pl.ANY),
            _resident(trin.shape),
            _resident(tric.shape),
        ],
        out_specs=pl.BlockSpec((td, width), lambda b: (b, 0)),
        scratch_shapes=[
            pltpu.VMEM((tc, heads, hd), _F32),
            pltpu.VMEM((tc, heads, hd), _F32),
            pltpu.SemaphoreType.DMA((2,)),
            pltpu.VMEM((SB_HEADS, td, HEAD_DIM), _F32),
            pltpu.VMEM((SB_HEADS, td, LANES), _F32),
        ],
        compiler_params=_params("arbitrary"),
        name="sb_cached",
    )(q_bf, kn, vn, cache_k, cache_v, trin, tric)


def _mem_body(q_ref, k_ref, v_ref, o_ref):
    for h in range(MEM_HEADS):
        cols = slice(h * HEAD_DIM, (h + 1) * HEAD_DIM)
        s = _nt_dot(q_ref[0, :, cols], k_ref[0, :, h, :].astype(_BF)) * ATT_SCALE
        e = jnp.exp(s - jnp.max(s, axis=1, keepdims=True))
        pr = e / jnp.sum(e, axis=1, keepdims=True)
        o_ref[0, :, cols] = _dot(pr.astype(_BF), v_ref[0, :, h, :].astype(_BF)).astype(o_ref.dtype)


def _mem_attend(qm_bf, mk, mv, tt):
    b, t, w = qm_bf.shape
    _, m, heads, hd = mk.shape
    return pl.pallas_call(
        _mem_body,
        out_shape=jax.ShapeDtypeStruct((b, t, w), _BF),
        grid=(b, t // tt),
        in_specs=[
            pl.BlockSpec((1, tt, w), lambda bi, i: (bi, i, 0)),
            pl.BlockSpec((1, m, heads, hd), lambda bi, i: (bi, 0, 0, 0)),
            pl.BlockSpec((1, m, heads, hd), lambda bi, i: (bi, 0, 0, 0)),
        ],
        out_specs=pl.BlockSpec((1, tt, w), lambda bi, i: (bi, i, 0)),
        compiler_params=_params("parallel", "parallel"),
        name="mem_attend",
    )(qm_bf, mk, mv)


def _layer_norm(r, g, b):
    mu = jnp.mean(r, axis=-1, keepdims=True)
    d = r - mu
    var = jnp.mean(d * d, axis=-1, keepdims=True)
    return d * lax.rsqrt(var + LN_EPS) * g + b


def _tail_body(x_ref, pool_ref, sb_ref, mem_ref, wo_ref, g_ref, b_ref, wq_ref, h_ref, hb_ref, pq_ref, *, alpha):
    o1, o2 = POOL_WIDTH, POOL_WIDTH + SB_WIDTH
    mix = (_dot(pool_ref[...], wo_ref[0:o1, :]) + _dot(sb_ref[...], wo_ref[o1:o2, :])
           + _dot(mem_ref[...], wo_ref[o2:, :]))
    h = _layer_norm(alpha * x_ref[...] + mix, g_ref[...], b_ref[...])
    h_ref[...] = h
    hb = h.astype(_BF)
    hb_ref[...] = hb
    for hd in range(PEER_HEADS):
        pq_ref[hd] = _dot(hb, wq_ref[:, hd * PEER_QDIM:(hd + 1) * PEER_QDIM]).astype(_BF)


def _tail(x, pool_o, sb_o, mem_o, wo_bf, g, b, wq_bf, alpha, tm):
    t = x.shape[0]
    row = lambda width: pl.BlockSpec((tm, width), lambda i: (i, 0))
    return pl.pallas_call(
        functools.partial(_tail_body, alpha=alpha),
        out_shape=[
            jax.ShapeDtypeStruct((t, D_MODEL), _F32),
            jax.ShapeDtypeStruct((t, D_MODEL), _BF),
            jax.ShapeDtypeStruct((PEER_HEADS, t, PEER_QDIM), _BF),
        ],
        grid=(t // tm,),
        in_specs=[row(D_MODEL), row(POOL_WIDTH), row(SB_WIDTH), row(MEM_WIDTH),
                  _resident(wo_bf.shape), _resident(g.shape), _resident(b.shape), _resident(wq_bf.shape)],
        out_specs=[row(D_MODEL), row(D_MODEL),
                   pl.BlockSpec((PEER_HEADS, tm, PEER_QDIM), lambda i: (0, i, 0))],
        compiler_params=_params("parallel"),
        name="tail",
    )(x, pool_o, sb_o, mem_o, wo_bf, g, b, wq_bf)


def _top_ranked(x, k):
    rows = lax.broadcasted_iota(jnp.int32, x.shape, 0).astype(_F32)
    work = x
    rank = jnp.full(x.shape, NOT_RANKED, _F32)
    vals = []
    for r in range(k):
        m = jnp.max(work, axis=0, keepdims=True)
        first = jnp.min(jnp.where(work == m, rows, float(x.shape[0])), axis=0, keepdims=True)
        sel = rows == first
        rank = jnp.where(sel, float(r), rank)
        work = jnp.where(sel, -jnp.inf, work)
        vals.append(m)
    return rank, jnp.concatenate(vals, axis=0)


def _top_ranked_untied(x, k):
    work = x
    rank = jnp.full(x.shape, NOT_RANKED, _F32)
    vals = []
    for r in range(k):
        m = jnp.max(work, axis=0, keepdims=True)
        sel = work == m
        rank = jnp.where(sel, float(r), rank)
        work = jnp.where(sel, -jnp.inf, work)
        vals.append(m)
    return rank, jnp.concatenate(vals, axis=0)


def _all_ranked_once(rank, k):
    n = jnp.sum(jnp.where(rank < NOT_RANKED, 1.0, 0.0), axis=0, keepdims=True)
    return jnp.max(jnp.abs(n - float(k))) == 0.0


_PAIR_ROW_LEN = tuple(PEER_TOPK // (a + 1) for a in range(PEER_TOPK))
_PAIR_PAD = -sum(_PAIR_ROW_LEN) % 8


def _score_body(pqp_ref, pqs_ref, sk1_ref, sk2_ref, r2_ref, e2_ref, cnt_ref, c1_ref,
                rank1_s, t1_s, rank2_s, t2_s, *, prompt_steps):
    k = PEER_TOPK
    tw = r2_ref.shape[2]
    crow = lax.broadcasted_iota(jnp.int32, (sum(_PAIR_ROW_LEN) + _PAIR_PAD, tw), 0).astype(_F32)
    from_prompt = pl.program_id(0) < prompt_steps

    def head(h, carry):
        pq = jnp.where(from_prompt, pqp_ref[h], pqs_ref[h])
        s1 = _nt_dot(sk1_ref[h], pq[:, :PEER_HALF])
        s2 = _nt_dot(sk2_ref[h], pq[:, PEER_HALF:])
        rank1_s[...], t1_s[...] = _top_ranked_untied(s1, k)
        rank2_s[...], t2_s[...] = _top_ranked_untied(s2, k)
        untied = jnp.logical_and(_all_ranked_once(rank1_s[...], k), _all_ranked_once(rank2_s[...], k))

        @pl.when(jnp.logical_not(untied))
        def _():
            rank1_s[...], t1_s[...] = _top_ranked(s1, k)
            rank2_s[...], t2_s[...] = _top_ranked(s2, k)

        rank1, t1, rank2, t2 = rank1_s[...], t1_s[...], rank2_s[...], t2_s[...]
        cand = jnp.concatenate(
            [t1[a:a + 1] + t2[0:n] for a, n in enumerate(_PAIR_ROW_LEN)]
            + [jnp.full((_PAIR_PAD, tw), -jnp.inf, _F32)], axis=0)
        picked = jnp.zeros(cand.shape, _F32)
        best0 = t1[0:1] + t2[0:1]
        z = jnp.zeros((1, tw), _F32)
        for _ in range(k):
            m = jnp.max(cand, axis=0, keepdims=True)
            first = jnp.min(jnp.where(cand == m, crow, float(cand.shape[0])), axis=0, keepdims=True)
            hit = crow == first
            picked = jnp.where(hit, 1.0, picked)
            cand = jnp.where(hit, -jnp.inf, cand)
            z = z + jnp.exp(m - best0)
        cnt = jnp.zeros((N_KEYS, tw), _F32)
        start = 0
        for a, n in enumerate(_PAIR_ROW_LEN):
            taken = jnp.sum(picked[start:start + n], axis=0, keepdims=True)
            cnt = jnp.where(rank1 == float(a), taken, cnt)
            start += n
        r2_ref[h] = rank2.astype(r2_ref.dtype)
        e2_ref[h] = jnp.exp(s2 - t2[0:1]).astype(e2_ref.dtype)
        cnt_ref[h] = cnt
        c1_ref[h] = jnp.exp(s1 - t1[0:1]) / z
        return carry

    lax.fori_loop(0, PEER_HEADS, head, 0)


def _score(pq_p, pq_s, sk1_bf, sk2_bf):
    tp, ts = pq_p.shape[1], pq_s.shape[1]
    t = tp + ts
    tw = SCORE_TILE
    np_ = tp // tw
    out = lambda dt: jax.ShapeDtypeStruct((PEER_HEADS, N_KEYS, t), dt)
    spec = pl.BlockSpec((PEER_HEADS, N_KEYS, tw), lambda i: (0, 0, i))
    return pl.pallas_call(
        functools.partial(_score_body, prompt_steps=np_),
        out_shape=[out(_BF), out(_BF), out(_F32), out(_F32)],
        grid=(t // tw,),
        in_specs=[pl.BlockSpec((PEER_HEADS, tw, PEER_QDIM), lambda i: (0, jnp.minimum(i, np_ - 1), 0)),
                  pl.BlockSpec((PEER_HEADS, tw, PEER_QDIM), lambda i: (0, jnp.maximum(i - np_, 0), 0)),
                  _resident(sk1_bf.shape), _resident(sk2_bf.shape)],
        out_specs=[spec] * 4,
        scratch_shapes=[pltpu.VMEM((N_KEYS, tw), _F32), pltpu.VMEM((PEER_TOPK, tw), _F32),
                        pltpu.VMEM((N_KEYS, tw), _F32), pltpu.VMEM((PEER_TOPK, tw), _F32)],
        compiler_params=_params("parallel"),
        name="peer_score",
    )(pq_p, pq_s, sk1_bf, sk2_bf)


def _gelu(a):
    return 0.5 * a * (1.0 + lax.erf(a * (1.0 / math.sqrt(2.0))))


def _activation_tile(pu_rows, hb_ref, act_ref):
    act_ref[...] = _gelu(_nt_dot(pu_rows, hb_ref[...])).astype(_BF)


def _gated_tile(act_ref, r2_ref, e2_ref, cnt_ref, c1_ref, tile):
    te, tm = act_ref.shape
    zero = jnp.zeros((), _BF)
    out = []
    for s in range(te // N_KEYS):
        i1 = tile * (te // N_KEYS) + s
        gate = jnp.zeros((N_KEYS, tm), _BF)
        for h in range(PEER_HEADS):
            thr = cnt_ref[h, pl.ds(i1, 1), :].astype(_BF)
            c1 = c1_ref[h, pl.ds(i1, 1), :].astype(_BF)
            gate = gate + jnp.where(r2_ref[h] < thr, e2_ref[h] * c1, zero)
        out.append(gate * act_ref[s * N_KEYS:(s + 1) * N_KEYS, :])
    return jnp.concatenate(out, axis=0)


def _expert_body(hb_ref, pu_ref, pvp_ref, pvc_ref, r2_ref, e2_ref, cnt_ref, c1_ref, o_ref,
                 acta_ref, actb_ref, *, te, pairs):
    jj = pl.program_id(1)
    aux = (r2_ref, e2_ref, cnt_ref, c1_ref)

    @pl.when(jj == 0)
    def _():
        _activation_tile(pu_ref[0:te, :], hb_ref, acta_ref)
        _activation_tile(pu_ref[te:2 * te, :], hb_ref, actb_ref)
        o_ref[...] = _dot(pvc_ref[...], _gated_tile(acta_ref, *aux, 0))

    @pl.when(jnp.logical_and(jj > 0, jj < pairs))
    def _():
        w_prev = _gated_tile(actb_ref, *aux, 2 * jj - 1)
        _activation_tile(pu_ref[0:te, :], hb_ref, acta_ref)
        o_ref[...] += _dot(pvp_ref[...], w_prev)
        w_cur = _gated_tile(acta_ref, *aux, 2 * jj)
        _activation_tile(pu_ref[te:2 * te, :], hb_ref, actb_ref)
        o_ref[...] += _dot(pvc_ref[...], w_cur)

    @pl.when(jj == pairs)
    def _():
        o_ref[...] += _dot(pvp_ref[...], _gated_tile(actb_ref, *aux, 2 * pairs - 1))


def _experts(hb, pu_bf, pvt_bf, r2, e2, cnt, c1, tm, te, token0):
    t = hb.shape[0]
    tiles = N_EXPERTS // te
    pairs = tiles // 2
    off = token0 // tm
    aux = pl.BlockSpec((PEER_HEADS, N_KEYS, tm), lambda i, j: (0, 0, i + off), pipeline_mode=pl.Buffered(1))
    return pl.pallas_call(
        functools.partial(_expert_body, te=te, pairs=pairs),
        out_shape=jax.ShapeDtypeStruct((D_MODEL, t), _F32),
        grid=(t // tm, pairs + 1),
        in_specs=[
            pl.BlockSpec((tm, D_MODEL), lambda i, j: (i, 0), pipeline_mode=pl.Buffered(1)),
            pl.BlockSpec((2 * te, D_MODEL), lambda i, j: (jnp.minimum(j, pairs - 1), 0)),
            pl.BlockSpec((None, D_MODEL, te), lambda i, j: (jnp.maximum(2 * j - 1, 0), 0, 0)),
            pl.BlockSpec((None, D_MODEL, te), lambda i, j: (jnp.minimum(2 * j, tiles - 1), 0, 0)),
            aux, aux, aux, aux,
        ],
        out_specs=pl.BlockSpec((D_MODEL, tm), lambda i, j: (0, i)),
        scratch_shapes=[pltpu.VMEM((te, tm), _BF), pltpu.VMEM((te, tm), _BF)],
        compiler_params=_params("parallel", "arbitrary"),
        name="peer_experts",
    )(hb, pu_bf, pvt_bf, pvt_bf, r2, e2, cnt, c1)


def _final_body(h_ref, ft_ref, g_ref, b_ref, o_ref, *, alpha):
    o_ref[...] = _layer_norm(alpha * h_ref[...] + ft_ref[...].T, g_ref[...], b_ref[...])


def _final_norm(h, f_t, g, b, alpha, tm):
    t = h.shape[0]
    return pl.pallas_call(
        functools.partial(_final_body, alpha=alpha),
        out_shape=jax.ShapeDtypeStruct((t, D_MODEL), _F32),
        grid=(t // tm,),
        in_specs=[pl.BlockSpec((tm, D_MODEL), lambda i: (i, 0)),
                  pl.BlockSpec((D_MODEL, tm), lambda i: (0, i)),
                  _resident(g.shape), _resident(b.shape)],
        out_specs=pl.BlockSpec((tm, D_MODEL), lambda i: (i, 0)),
        compiler_params=_params("parallel"),
        name="final_norm",
    )(h, f_t, g, b)


_IN_PLAN = (
    (0, POOL_WIDTH, (_F32,)),
    (POOL_WIDTH, POOL_WIDTH + SB_WIDTH, (_BF,)),
    (POOL_WIDTH + SB_WIDTH, POOL_WIDTH + 2 * SB_WIDTH, (_F32, _BF)),
    (POOL_WIDTH + 2 * SB_WIDTH, POOL_WIDTH + 3 * SB_WIDTH, (_F32, _BF)),
    (POOL_WIDTH + 3 * SB_WIDTH, IN_WIDTH, (_BF,)),
)
_MEM_PLAN = ((0, MEM_WIDTH, (_F32,)), (MEM_WIDTH, 2 * MEM_WIDTH, (_F32,)))


def kernel(x_prompt, x_sample, state_pool, cache_sb_k, cache_sb_v, cache_mem_k, cache_mem_v, mem_prompt, w_in, w_pool, pool_scale, w_mem_kv, w_out, ln1_g, ln1_b, peer_wq, peer_subkey1, peer_subkey2, peer_u, peer_v, ln2_g, ln2_b):
    depth = w_in.shape[0]
    bp, seq, _ = x_prompt.shape
    bs, dec, _ = x_sample.shape
    past = cache_sb_k.shape[2]
    n_mem = mem_prompt.shape[1]
    assert bp == 1, "the prompt group is one stream"
    alpha = (2.0 * depth) ** 0.25
    tp, ts = seq * bp, bs * dec

    xp = x_prompt.reshape(tp, D_MODEL)
    xs = x_sample.reshape(ts, D_MODEL)
    outs = {k: [] for k in ("pool_p", "sbk_p", "sbv_p", "mk_p", "mv_p", "pool_s", "sbk_s", "sbv_s")}
    def layer(a, l):
        return a.reshape(a.shape[1:]) if a.shape[0] == 1 else a[l]

    for l in range(depth):
        w_in_bf = layer(w_in, l).astype(_BF)
        wp_bf = layer(w_pool, l).astype(_BF)
        scale = layer(pool_scale, l).reshape(1, POOL_WIDTH)
        wo_bf = layer(w_out, l).astype(_BF)
        wq_bf = layer(peer_wq, l).astype(_BF)
        sk1_bf = layer(peer_subkey1, l).astype(_BF)
        sk2_bf = layer(peer_subkey2, l).astype(_BF)
        pu_bf = layer(peer_u, l).astype(_BF)
        pvt_bf = jnp.swapaxes(layer(peer_v, l).reshape(N_EXPERTS // EXPERT_TILE, EXPERT_TILE, D_MODEL),
                              1, 2).astype(_BF)
        g1, b1 = layer(ln1_g, l).reshape(1, D_MODEL), layer(ln1_b, l).reshape(1, D_MODEL)
        g2, b2 = layer(ln2_g, l).reshape(1, D_MODEL), layer(ln2_b, l).reshape(1, D_MODEL)
        hist_s0 = layer(state_pool, l)

        p, q, k, kb, v, vb, qm = _project(xp, w_in_bf, _IN_PLAN, ROW_TILE)
        p3 = p.reshape(bp, seq, POOL_WIDTH)
        pool_o = _pool_mix(p3, jnp.zeros((bp, POOL_HIST, POOL_WIDTH), _F32), wp_bf, scale, 0, TOKEN_TILE)
        sb_o = _sb_prompt(q, kb, vb, SB_TILE, SB_HEADS_PER_STEP)
        mk, mv = _project(mem_prompt.reshape(bp * n_mem, D_MODEL), layer(w_mem_kv, l).astype(_BF), _MEM_PLAN,
                          ROW_TILE)
        mk = mk.reshape(bp, n_mem, MEM_HEADS, HEAD_DIM)
        mv = mv.reshape(bp, n_mem, MEM_HEADS, HEAD_DIM)
        mem_o = _mem_attend(qm.reshape(bp, seq, MEM_WIDTH), mk, mv, TOKEN_TILE)
        h_p, hb_p, pq_p = _tail(xp, pool_o.reshape(tp, POOL_WIDTH), sb_o, mem_o.reshape(tp, MEM_WIDTH),
                                wo_bf, g1, b1, wq_bf, alpha, ROW_TILE)
        outs["pool_p"].append(p3[:, seq - POOL_HIST:])
        outs["sbk_p"].append(k.reshape(bp, seq, SB_HEADS, HEAD_DIM))
        outs["sbv_p"].append(v.reshape(bp, seq, SB_HEADS, HEAD_DIM))
        outs["mk_p"].append(mk)
        outs["mv_p"].append(mv)

        p, q, k, kb, v, vb, qm = _project(xs, w_in_bf, _IN_PLAN, ROW_TILE)
        p3 = p.reshape(bs, dec, POOL_WIDTH)
        pool_o = _pool_mix(p3, hist_s0, wp_bf, scale, past, dec)
        sb_o = _sb_cached(q, kb.reshape(bs, dec, SB_WIDTH), vb.reshape(bs, dec, SB_WIDTH),
                          layer(cache_sb_k, l), layer(cache_sb_v, l), dec, SB_TILE)
        mem_o = _mem_attend(qm.reshape(bs, dec, MEM_WIDTH), layer(cache_mem_k, l), layer(cache_mem_v, l), dec)
        h_s, hb_s, pq_s = _tail(xs, pool_o.reshape(ts, POOL_WIDTH), sb_o, mem_o.reshape(ts, MEM_WIDTH),
                                wo_bf, g1, b1, wq_bf, alpha, ROW_TILE)
        hist_s = jnp.concatenate([hist_s0, p3], axis=1)[:, -POOL_HIST:]
        outs["pool_s"].append(hist_s)
        outs["sbk_s"].append(k.reshape(bs, dec, SB_HEADS, HEAD_DIM))
        outs["sbv_s"].append(v.reshape(bs, dec, SB_HEADS, HEAD_DIM))

        r2, e2, cnt, c1 = _score(pq_p, pq_s, sk1_bf, sk2_bf)
        ft_p = _experts(hb_p, pu_bf, pvt_bf, r2, e2, cnt, c1, EXPERT_TOKEN_TILE, EXPERT_TILE, 0)
        ft_s = _experts(hb_s, pu_bf, pvt_bf, r2, e2, cnt, c1, min(ts, EXPERT_TOKEN_TILE), EXPERT_TILE, tp)
        xp = _final_norm(h_p, ft_p, g2, b2, alpha, TOKEN_TILE)
        xs = _final_norm(h_s, ft_s, g2, b2, alpha, TOKEN_TILE)

    st = lambda name: jnp.stack(outs[name])
    return (xp.reshape(bp, seq, D_MODEL), xs.reshape(bs, dec, D_MODEL),
            st("pool_p"), st("sbk_p"), st("sbv_p"), st("mk_p"), st("mv_p"),
            st("pool_s"), st("sbk_s"), st("sbv_s"))
```

```python
import functools
import math

import jax
import jax.numpy as jnp
from jax import lax
from jax.experimental import pallas as pl
from jax.experimental.pallas import tpu as pltpu

_BF = jnp.bfloat16
_F32 = jnp.float32

D_MODEL = 2048
HEAD_DIM = 128
LANES = 128
POOL_WIDTH = D_MODEL // 4
POOL_WINDOWS = (2, 4, 8, 16)
POOL_GROUP_DIM = POOL_WIDTH // len(POOL_WINDOWS)
POOL_HIST = max(POOL_WINDOWS) - 1
POOL_HALO = POOL_HIST + 1
SB_WIDTH = D_MODEL // 2
SB_HEADS = SB_WIDTH // HEAD_DIM
MEM_WIDTH = D_MODEL // 4
MEM_HEADS = MEM_WIDTH // HEAD_DIM
IN_WIDTH = POOL_WIDTH + 3 * SB_WIDTH + MEM_WIDTH
PEER_HEADS = 8
N_KEYS = 128
N_EXPERTS = N_KEYS * N_KEYS
PEER_QDIM = 256
PEER_HALF = PEER_QDIM // 2
PEER_TOPK = 16
LN_EPS = 1e-5
ATT_SCALE = 1.0 / math.sqrt(HEAD_DIM)
SB_ZERO_LOG = -104.0
NOT_RANKED = 99.0

VMEM_LIMIT = 56 * 1024 * 1024

ROW_TILE = 256
TOKEN_TILE = 512
EXPERT_TILE = 512
EXPERT_TOKEN_TILE = 1024
SB_TILE = 256
SB_HEADS_PER_STEP = 4
SCORE_TILE = 512


def _params(*sem):
    return pltpu.CompilerParams(dimension_semantics=sem, vmem_limit_bytes=VMEM_LIMIT)


def _nt_dot(a, b):
    return lax.dot_general(a, b, (((1,), (1,)), ((), ())), preferred_element_type=_F32)


def _dot(a, b):
    return jnp.dot(a, b, preferred_element_type=_F32)


def _resident(shape):
    nd = len(shape)
    return pl.BlockSpec(shape, lambda *_: (0,) * nd, pipeline_mode=pl.Buffered(1))


def _proj_body(x_ref, w_ref, *o_refs, plan):
    xb = x_ref[...].astype(_BF)
    n = 0
    for a, b, dtypes in plan:
        y = _dot(xb, w_ref[:, a:b])
        for dt in dtypes:
            o_refs[n][...] = y.astype(dt)
            n += 1


def _project(x, w_bf, plan, tm):
    t, k = x.shape
    n = w_bf.shape[1]
    shapes, specs = [], []
    for a, b, dtypes in plan:
        for dt in dtypes:
            shapes.append(jax.ShapeDtypeStruct((t, b - a), dt))
            specs.append(pl.BlockSpec((tm, b - a), lambda i: (i, 0)))
    return pl.pallas_call(
        functools.partial(_proj_body, plan=plan),
        out_shape=shapes,
        grid=(t // tm,),
        in_specs=[pl.BlockSpec((tm, k), lambda i: (i, 0)), _resident((k, n))],
        out_specs=specs,
        compiler_params=_params("parallel"),
        name="project",
    )(x, w_bf)


def _pool_body(p_ref, prev_ref, hist_ref, wp_ref, sc_ref, o_ref, buf_ref, *, tt, pos0):
    i = pl.program_id(1)
    buf_ref[0:POOL_HALO, :] = jnp.where(i == 0, hist_ref[0], prev_ref[0])
    buf_ref[POOL_HALO:POOL_HALO + tt, :] = p_ref[0]
    pos = pos0 + i * tt + lax.broadcasted_iota(jnp.int32, (tt, POOL_GROUP_DIM), 0)
    for g, w in enumerate(POOL_WINDOWS):
        cols = slice(g * POOL_GROUP_DIM, (g + 1) * POOL_GROUP_DIM)
        win = buf_ref[POOL_HALO - (w - 1):POOL_HALO - (w - 1) + tt, cols]
        for back in range(w - 2, -1, -1):
            win = win + buf_ref[POOL_HALO - back:POOL_HALO - back + tt, cols]
        cnt = jnp.minimum(pos + 1, w).astype(_F32)
        pooled = win / cnt - p_ref[0, :, cols]
        mixed = _dot(pooled.astype(_BF), wp_ref[g]) * sc_ref[:, cols]
        o_ref[0, :, cols] = mixed.astype(o_ref.dtype)


def _pool_mix(p, hist, wp_bf, scale, pos0, tt):
    b, t, c = p.shape
    hist16 = jnp.pad(hist, ((0, 0), (POOL_HALO - POOL_HIST, 0), (0, 0)))
    per = tt // POOL_HALO
    return pl.pallas_call(
        functools.partial(_pool_body, tt=tt, pos0=pos0),
        out_shape=jax.ShapeDtypeStruct((b, t, c), _BF),
        grid=(b, t // tt),
        in_specs=[
            pl.BlockSpec((1, tt, c), lambda bi, i: (bi, i, 0)),
            pl.BlockSpec((1, POOL_HALO, c), lambda bi, i: (bi, jnp.maximum(i * per - 1, 0), 0)),
            pl.BlockSpec((1, POOL_HALO, c), lambda bi, i: (bi, 0, 0)),
            _resident(wp_bf.shape),
            _resident(scale.shape),
        ],
        out_specs=pl.BlockSpec((1, tt, c), lambda bi, i: (bi, i, 0)),
        scratch_shapes=[pltpu.VMEM((POOL_HALO + tt, c), _F32)],
        compiler_params=_params("parallel", "parallel"),
        name="pool_mix",
    )(p, p, hist16, wp_bf, scale)


def _tri_ext(tk):
    j = lax.broadcasted_iota(jnp.int32, (tk, tk + LANES), 0)
    s = lax.broadcasted_iota(jnp.int32, (tk, tk + LANES), 1)
    return jnp.where((j > s) | (s >= tk), 1.0, 0.0).astype(_BF)


def _sb_block(q, kblk, vblk, c_b, tri, mask):
    tk = kblk.shape[0]
    z = _nt_dot(q, kblk) * ATT_SCALE
    softplus = jnp.maximum(z, 0.0) + jnp.log1p(jnp.exp(-jnp.abs(z)))
    log_keep = -softplus
    if mask is not None:
        log_keep = jnp.where(mask, log_keep, 0.0)
    hi = log_keep.astype(_BF)
    lo = (log_keep - hi.astype(_F32)).astype(_BF)
    ext = _dot(hi, tri) + _dot(lo, tri)
    between, total = ext[:, :tk], ext[:, tk:]
    newer = c_b if tk == LANES else jnp.concatenate([c_b] * (tk // LANES), axis=1)
    a = jnp.exp((z - softplus) + between + newer)
    if mask is not None:
        a = jnp.where(mask, a, 0.0)
    return _dot(a.astype(_BF), vblk), c_b + total


def _causal_mask(tq, tk):
    row = lax.broadcasted_iota(jnp.int32, (tq, tk), 0)
    col = lax.broadcasted_iota(jnp.int32, (tq, tk), 1)
    return col < row


def _sb_prompt_body(q_ref, k_ref, v_ref, tri_ref, o_ref, acc_ref, c_ref, *, tq, nh):
    i = pl.program_id(1)
    tri = tri_ref[...]
    heads = [slice(h * HEAD_DIM, (h + 1) * HEAD_DIM) for h in range(nh)]

    def step(j, mask, first):
        rows = pl.ds(pl.multiple_of(j * tq, tq), tq)
        top = None
        for h, cols in enumerate(heads):
            c_b = jnp.zeros((tq, LANES), _F32) if first else c_ref[h]
            pv, c = _sb_block(q_ref[:, cols], k_ref[rows, cols], v_ref[rows, cols], c_b, tri, mask)
            acc_ref[h] = pv if first else acc_ref[h] + pv
            c_ref[h] = c
            top = jnp.max(c) if top is None else jnp.maximum(top, jnp.max(c))
        return top < SB_ZERO_LOG

    def cond(state):
        j, done = state
        return jnp.logical_and(j >= 0, jnp.logical_not(done))

    def body(state):
        j, _ = state
        return j - 1, step(j, None, False)

    lax.while_loop(cond, body, (i - 1, step(i, _causal_mask(tq, tq), True)))
    for h, cols in enumerate(heads):
        o_ref[:, cols] = acc_ref[h].astype(o_ref.dtype)


def _sb_prompt(q_bf, k_bf, v_bf, tq, nh):
    t = q_bf.shape[0]
    tri = _tri_ext(tq)
    w = nh * HEAD_DIM
    return pl.pallas_call(
        functools.partial(_sb_prompt_body, tq=tq, nh=nh),
        out_shape=jax.ShapeDtypeStruct((t, SB_WIDTH), _BF),
        grid=(SB_HEADS // nh, t // tq),
        in_specs=[
            pl.BlockSpec((tq, w), lambda g, i: (i, g)),
            pl.BlockSpec((t, w), lambda g, i: (0, g), pipeline_mode=pl.Buffered(1)),
            pl.BlockSpec((t, w), lambda g, i: (0, g), pipeline_mode=pl.Buffered(1)),
            _resident(tri.shape),
        ],
        out_specs=pl.BlockSpec((tq, w), lambda g, i: (i, g)),
        scratch_shapes=[pltpu.VMEM((nh, tq, HEAD_DIM), _F32), pltpu.VMEM((nh, tq, LANES), _F32)],
        compiler_params=_params("parallel", "parallel"),
        name="sb_prompt",
    )(q_bf, k_bf, v_bf, tri)


def _sb_cached_body(q_ref, kn_ref, vn_ref, ck_ref, cv_ref, trin_ref, tric_ref, o_ref,
                    kbuf, vbuf, sem, acc_ref, c_ref, *, td, tn, tc, n_blocks):
    b = pl.program_id(0)
    mask = _causal_mask(td, tn)
    trin = trin_ref[...]
    for h in range(SB_HEADS):
        cols = slice(h * HEAD_DIM, (h + 1) * HEAD_DIM)
        pv, c = _sb_block(q_ref[:, cols], kn_ref[0, :, cols], vn_ref[0, :, cols],
                          jnp.zeros((td, LANES), _F32), trin, mask)
        acc_ref[h] = pv
        c_ref[h] = c

    def copies(j):
        rows = pl.ds(pl.multiple_of(j * tc, tc), tc)
        return (pltpu.make_async_copy(ck_ref.at[b, rows], kbuf, sem.at[0]),
                pltpu.make_async_copy(cv_ref.at[b, rows], vbuf, sem.at[1]))

    def cond(state):
        j, done = state
        return jnp.logical_and(j >= 0, jnp.logical_not(done))

    def body(state):
        j, _ = state
        ck, cv = copies(j)
        ck.start()
        cv.start()
        ck.wait()
        cv.wait()
        tric = tric_ref[...]
        top = jnp.float32(-jnp.inf)
        for h in range(SB_HEADS):
            cols = slice(h * HEAD_DIM, (h + 1) * HEAD_DIM)
            pv, c = _sb_block(q_ref[:, cols], kbuf[:, h, :].astype(_BF), vbuf[:, h, :].astype(_BF),
                              c_ref[h], tric, None)
            acc_ref[h] += pv
            c_ref[h] = c
            top = jnp.maximum(top, jnp.max(c))
        return j - 1, top < SB_ZERO_LOG

    lax.while_loop(cond, body, (jnp.int32(n_blocks - 1), jnp.max(c_ref[...]) < SB_ZERO_LOG))
    for h in range(SB_HEADS):
        o_ref[:, h * HEAD_DIM:(h + 1) * HEAD_DIM] = acc_ref[h].astype(o_ref.dtype)


def _sb_cached(q_bf, k_new, v_new, cache_k, cache_v, td, tc):
    nb, past, heads, hd = cache_k.shape
    width = heads * hd
    tn = LANES
    pad = ((0, 0), (0, tn - td), (0, 0))
    kn, vn = jnp.pad(k_new, pad), jnp.pad(v_new, pad)
    trin, tric = _tri_ext(tn), _tri_ext(tc)
    return pl.pallas_call(
        functools.partial(_sb_cached_body, td=td, tn=tn, tc=tc, n_blocks=past // tc),
        out_shape=jax.ShapeDtypeStruct((nb * td, width), _BF),
        grid=(nb,),
        in_specs=[
            pl.BlockSpec((td, width), lambda b: (b, 0)),
            pl.BlockSpec((1, tn, width), lambda b: (b, 0, 0)),
            pl.BlockSpec((1, tn, width), lambda b: (b, 0, 0)),
            pl.BlockSpec(memory_space=pl.ANY),
            pl.BlockSpec(memory_space=pl.ANY),
            _resident(trin.shape),
            _resident(tric.shape),
        ],
        out_specs=pl.BlockSpec((td, width), lambda b: (b, 0)),
        scratch_shapes=[
            pltpu.VMEM((tc, heads, hd), _F32),
            pltpu.VMEM((tc, heads, hd), _F32),
            pltpu.SemaphoreType.DMA((2,)),
            pltpu.VMEM((SB_HEADS, td, HEAD_DIM), _F32),
            pltpu.VMEM((SB_HEADS, td, LANES), _F32),
        ],
        compiler_params=_params("arbitrary"),
        name="sb_cached",
    )(q_bf, kn, vn, cache_k, cache_v, trin, tric)


def _mem_body(q_ref, k_ref, v_ref, o_ref):
    for h in range(MEM_HEADS):
        cols = slice(h * HEAD_DIM, (h + 1) * HEAD_DIM)
        s = _nt_dot(q_ref[0, :, cols], k_ref[0, :, h, :].astype(_BF)) * ATT_SCALE
        e = jnp.exp(s - jnp.max(s, axis=1, keepdims=True))
        pr = e / jnp.sum(e, axis=1, keepdims=True)
        o_ref[0, :, cols] = _dot(pr.astype(_BF), v_ref[0, :, h, :].astype(_BF)).astype(o_ref.dtype)


def _mem_attend(qm_bf, mk, mv, tt):
    b, t, w = qm_bf.shape
    _, m, heads, hd = mk.shape
    return pl.pallas_call(
        _mem_body,
        out_shape=jax.ShapeDtypeStruct((b, t, w), _BF),
        grid=(b, t // tt),
        in_specs=[
            pl.BlockSpec((1, tt, w), lambda bi, i: (bi, i, 0)),
            pl.BlockSpec((1, m, heads, hd), lambda bi, i: (bi, 0, 0, 0)),
            pl.BlockSpec((1, m, heads, hd), lambda bi, i: (bi, 0, 0, 0)),
        ],
        out_specs=pl.BlockSpec((1, tt, w), lambda bi, i: (bi, i, 0)),
        compiler_params=_params("parallel", "parallel"),
        name="mem_attend",
    )(qm_bf, mk, mv)


def _layer_norm(r, g, b):
    mu = jnp.mean(r, axis=-1, keepdims=True)
    d = r - mu
    var = jnp.mean(d * d, axis=-1, keepdims=True)
    return d * lax.rsqrt(var + LN_EPS) * g + b


def _tail_body(x_ref, pool_ref, sb_ref, mem_ref, wo_ref, g_ref, b_ref, wq_ref, h_ref, hb_ref, pq_ref, *, alpha):
    o1, o2 = POOL_WIDTH, POOL_WIDTH + SB_WIDTH
    mix = (_dot(pool_ref[...], wo_ref[0:o1, :]) + _dot(sb_ref[...], wo_ref[o1:o2, :])
           + _dot(mem_ref[...], wo_ref[o2:, :]))
    h = _layer_norm(alpha * x_ref[...] + mix, g_ref[...], b_ref[...])
    h_ref[...] = h
    hb = h.astype(_BF)
    hb_ref[...] = hb
    for hd in range(PEER_HEADS):
        pq_ref[hd] = _dot(hb, wq_ref[:, hd * PEER_QDIM:(hd + 1) * PEER_QDIM]).astype(_BF)


def _tail(x, pool_o, sb_o, mem_o, wo_bf, g, b, wq_bf, alpha, tm):
    t = x.shape[0]
    row = lambda width: pl.BlockSpec((tm, width), lambda i: (i, 0))
    return pl.pallas_call(
        functools.partial(_tail_body, alpha=alpha),
        out_shape=[
            jax.ShapeDtypeStruct((t, D_MODEL), _F32),
            jax.ShapeDtypeStruct((t, D_MODEL), _BF),
            jax.ShapeDtypeStruct((PEER_HEADS, t, PEER_QDIM), _BF),
        ],
        grid=(t // tm,),
        in_specs=[row(D_MODEL), row(POOL_WIDTH), row(SB_WIDTH), row(MEM_WIDTH),
                  _resident(wo_bf.shape), _resident(g.shape), _resident(b.shape), _resident(wq_bf.shape)],
        out_specs=[row(D_MODEL), row(D_MODEL),
                   pl.BlockSpec((PEER_HEADS, tm, PEER_QDIM), lambda i: (0, i, 0))],
        compiler_params=_params("parallel"),
        name="tail",
    )(x, pool_o, sb_o, mem_o, wo_bf, g, b, wq_bf)


def _top_ranked(x, k):
    rows = lax.broadcasted_iota(jnp.int32, x.shape, 0).astype(_F32)
    work = x
    rank = jnp.full(x.shape, NOT_RANKED, _F32)
    vals = []
    for r in range(k):
        m = jnp.max(work, axis=0, keepdims=True)
        first = jnp.min(jnp.where(work == m, rows, float(x.shape[0])), axis=0, keepdims=True)
        sel = rows == first
        rank = jnp.where(sel, float(r), rank)
        work = jnp.where(sel, -jnp.inf, work)
        vals.append(m)
    return rank, jnp.concatenate(vals, axis=0)


def _top_ranked_untied(x, k, with_rank):
    work = x
    rank = jnp.full(x.shape, NOT_RANKED, _F32) if with_rank else None
    vals = []
    for r in range(k):
        m = jnp.max(work, axis=0, keepdims=True)
        sel = work == m
        if with_rank:
            rank = jnp.where(sel, float(r), rank)
        work = jnp.where(sel, -jnp.inf, work)
        vals.append(m)
    removed = jnp.sum(jnp.where(work == -jnp.inf, 1.0, 0.0), axis=0, keepdims=True)
    return rank, jnp.concatenate(vals, axis=0), jnp.max(jnp.abs(removed - float(k))) == 0.0


_PAIR_ROW_LEN = tuple(PEER_TOPK // (a + 1) for a in range(PEER_TOPK))
_PAIR_PAD = -sum(_PAIR_ROW_LEN) % 8


def _score_body(pqp_ref, pqs_ref, sk1_ref, sk2_ref, r2_ref, e2_ref, cnt_ref, c1_ref,
                key1_s, ref1_s, t1_s, rank2_s, t2_s, *, prompt_steps):
    k = PEER_TOPK
    tw = r2_ref.shape[2]
    crow = lax.broadcasted_iota(jnp.int32, (sum(_PAIR_ROW_LEN) + _PAIR_PAD, tw), 0).astype(_F32)
    from_prompt = pl.program_id(0) < prompt_steps

    def head(h, carry):
        pq = jnp.where(from_prompt, pqp_ref[h], pqs_ref[h])
        s1 = _nt_dot(sk1_ref[h], pq[:, :PEER_HALF])
        s2 = _nt_dot(sk2_ref[h], pq[:, PEER_HALF:])
        _, t1_s[...], untied1 = _top_ranked_untied(s1, k, False)
        key1_s[...] = s1
        ref1_s[...] = t1_s[...]
        rank2_s[...], t2_s[...], untied2 = _top_ranked_untied(s2, k, True)

        @pl.when(jnp.logical_not(jnp.logical_and(untied1, untied2)))
        def _():
            key1_s[...], t1_s[...] = _top_ranked(s1, k)
            ref1_s[...] = lax.broadcasted_iota(jnp.int32, (k, tw), 0).astype(_F32)
            rank2_s[...], t2_s[...] = _top_ranked(s2, k)

        key1, ref1, t1, rank2, t2 = key1_s[...], ref1_s[...], t1_s[...], rank2_s[...], t2_s[...]
        cand = jnp.concatenate(
            [t1[a:a + 1] + t2[0:n] for a, n in enumerate(_PAIR_ROW_LEN)]
            + [jnp.full((_PAIR_PAD, tw), -jnp.inf, _F32)], axis=0)
        picked = jnp.zeros(cand.shape, _F32)
        best0 = t1[0:1] + t2[0:1]
        z = jnp.zeros((1, tw), _F32)
        for _ in range(k):
            m = jnp.max(cand, axis=0, keepdims=True)
            first = jnp.min(jnp.where(cand == m, crow, float(cand.shape[0])), axis=0, keepdims=True)
            hit = crow == first
            picked = jnp.where(hit, 1.0, picked)
            cand = jnp.where(hit, -jnp.inf, cand)
            z = z + jnp.exp(m - best0)
        cnt = jnp.zeros((N_KEYS, tw), _F32)
        start = 0
        for a, n in enumerate(_PAIR_ROW_LEN):
            taken = jnp.sum(picked[start:start + n], axis=0, keepdims=True)
            cnt = jnp.where(key1 == ref1[a:a + 1], taken, cnt)
            start += n
        r2_ref[h] = rank2.astype(r2_ref.dtype)
        e2_ref[h] = jnp.exp(s2 - t2[0:1]).astype(e2_ref.dtype)
        cnt_ref[h] = cnt
        c1_ref[h] = jnp.exp(s1 - t1[0:1]) / z
        return carry

    lax.fori_loop(0, PEER_HEADS, head, 0)


def _score(pq_p, pq_s, sk1_bf, sk2_bf):
    tp, ts = pq_p.shape[1], pq_s.shape[1]
    t = tp + ts
    tw = SCORE_TILE
    np_ = tp // tw
    out = lambda dt: jax.ShapeDtypeStruct((PEER_HEADS, N_KEYS, t), dt)
    spec = pl.BlockSpec((PEER_HEADS, N_KEYS, tw), lambda i: (0, 0, i))
    return pl.pallas_call(
        functools.partial(_score_body, prompt_steps=np_),
        out_shape=[out(_BF), out(_BF), out(_F32), out(_F32)],
        grid=(t // tw,),
        in_specs=[pl.BlockSpec((PEER_HEADS, tw, PEER_QDIM), lambda i: (0, jnp.minimum(i, np_ - 1), 0)),
                  pl.BlockSpec((PEER_HEADS, tw, PEER_QDIM), lambda i: (0, jnp.maximum(i - np_, 0), 0)),
                  _resident(sk1_bf.shape), _resident(sk2_bf.shape)],
        out_specs=[spec] * 4,
        scratch_shapes=[pltpu.VMEM((N_KEYS, tw), _F32), pltpu.VMEM((PEER_TOPK, tw), _F32),
                        pltpu.VMEM((PEER_TOPK, tw), _F32),
                        pltpu.VMEM((N_KEYS, tw), _F32), pltpu.VMEM((PEER_TOPK, tw), _F32)],
        compiler_params=_params("parallel"),
        name="peer_score",
    )(pq_p, pq_s, sk1_bf, sk2_bf)


def _gelu(a):
    return 0.5 * a * (1.0 + lax.erf(a * (1.0 / math.sqrt(2.0))))


def _activation_tile(pu_rows, hb_ref, act_ref):
    act_ref[...] = _gelu(_nt_dot(pu_rows, hb_ref[...])).astype(_BF)


def _gated_tile(act_ref, r2_ref, e2_ref, cnt_ref, c1_ref, tile):
    te, tm = act_ref.shape
    zero = jnp.zeros((), _BF)
    out = []
    for s in range(te // N_KEYS):
        i1 = tile * (te // N_KEYS) + s
        gate = jnp.zeros((N_KEYS, tm), _BF)
        for h in range(PEER_HEADS):
            thr = cnt_ref[h, pl.ds(i1, 1), :].astype(_BF)
            c1 = c1_ref[h, pl.ds(i1, 1), :].astype(_BF)
            gate = gate + jnp.where(r2_ref[h] < thr, e2_ref[h] * c1, zero)
        out.append(gate * act_ref[s * N_KEYS:(s + 1) * N_KEYS, :])
    return jnp.concatenate(out, axis=0)


def _expert_body(hb_ref, pu_ref, pvp_ref, pvc_ref, r2_ref, e2_ref, cnt_ref, c1_ref, o_ref,
                 acta_ref, actb_ref, *, te, pairs):
    jj = pl.program_id(1)
    aux = (r2_ref, e2_ref, cnt_ref, c1_ref)

    @pl.when(jj == 0)
    def _():
        _activation_tile(pu_ref[0:te, :], hb_ref, acta_ref)
        _activation_tile(pu_ref[te:2 * te, :], hb_ref, actb_ref)
        o_ref[...] = _dot(pvc_ref[...], _gated_tile(acta_ref, *aux, 0))

    @pl.when(jnp.logical_and(jj > 0, jj < pairs))
    def _():
        w_prev = _gated_tile(actb_ref, *aux, 2 * jj - 1)
        _activation_tile(pu_ref[0:te, :], hb_ref, acta_ref)
        o_ref[...] += _dot(pvp_ref[...], w_prev)
        w_cur = _gated_tile(acta_ref, *aux, 2 * jj)
        _activation_tile(pu_ref[te:2 * te, :], hb_ref, actb_ref)
        o_ref[...] += _dot(pvc_ref[...], w_cur)

    @pl.when(jj == pairs)
    def _():
        o_ref[...] += _dot(pvp_ref[...], _gated_tile(actb_ref, *aux, 2 * pairs - 1))


def _experts(hb, pu_bf, pvt_bf, r2, e2, cnt, c1, tm, te, token0):
    t = hb.shape[0]
    tiles = N_EXPERTS // te
    pairs = tiles // 2
    off = token0 // tm
    aux = pl.BlockSpec((PEER_HEADS, N_KEYS, tm), lambda i, j: (0, 0, i + off), pipeline_mode=pl.Buffered(1))
    return pl.pallas_call(
        functools.partial(_expert_body, te=te, pairs=pairs),
        out_shape=jax.ShapeDtypeStruct((D_MODEL, t), _F32),
        grid=(t // tm, pairs + 1),
        in_specs=[
            pl.BlockSpec((tm, D_MODEL), lambda i, j: (i, 0), pipeline_mode=pl.Buffered(1)),
            pl.BlockSpec((2 * te, D_MODEL), lambda i, j: (jnp.minimum(j, pairs - 1), 0)),
            pl.BlockSpec((None, D_MODEL, te), lambda i, j: (jnp.maximum(2 * j - 1, 0), 0, 0)),
            pl.BlockSpec((None, D_MODEL, te), lambda i, j: (jnp.minimum(2 * j, tiles - 1), 0, 0)),
            aux, aux, aux, aux,
        ],
        out_specs=pl.BlockSpec((D_MODEL, tm), lambda i, j: (0, i)),
        scratch_shapes=[pltpu.VMEM((te, tm), _BF), pltpu.VMEM((te, tm), _BF)],
        compiler_params=_params("parallel", "arbitrary"),
        name="peer_experts",
    )(hb, pu_bf, pvt_bf, pvt_bf, r2, e2, cnt, c1)


def _final_body(h_ref, ft_ref, g_ref, b_ref, o_ref, *, alpha):
    o_ref[...] = _layer_norm(alpha * h_ref[...] + ft_ref[...].T, g_ref[...], b_ref[...])


def _final_norm(h, f_t, g, b, alpha, tm):
    t = h.shape[0]
    return pl.pallas_call(
        functools.partial(_final_body, alpha=alpha),
        out_shape=jax.ShapeDtypeStruct((t, D_MODEL), _F32),
        grid=(t // tm,),
        in_specs=[pl.BlockSpec((tm, D_MODEL), lambda i: (i, 0)),
                  pl.BlockSpec((D_MODEL, tm), lambda i: (0, i)),
                  _resident(g.shape), _resident(b.shape)],
        out_specs=pl.BlockSpec((tm, D_MODEL), lambda i: (i, 0)),
        compiler_params=_params("parallel"),
        name="final_norm",
    )(h, f_t, g, b)


_IN_PLAN = (
    (0, POOL_WIDTH, (_F32,)),
    (POOL_WIDTH, POOL_WIDTH + SB_WIDTH, (_BF,)),
    (POOL_WIDTH + SB_WIDTH, POOL_WIDTH + 2 * SB_WIDTH, (_F32, _BF)),
    (POOL_WIDTH + 2 * SB_WIDTH, POOL_WIDTH + 3 * SB_WIDTH, (_F32, _BF)),
    (POOL_WIDTH + 3 * SB_WIDTH, IN_WIDTH, (_BF,)),
)
_MEM_PLAN = ((0, MEM_WIDTH, (_F32,)), (MEM_WIDTH, 2 * MEM_WIDTH, (_F32,)))


def kernel(x_prompt, x_sample, state_pool, cache_sb_k, cache_sb_v, cache_mem_k, cache_mem_v, mem_prompt, w_in, w_pool, pool_scale, w_mem_kv, w_out, ln1_g, ln1_b, peer_wq, peer_subkey1, peer_subkey2, peer_u, peer_v, ln2_g, ln2_b):
    depth = w_in.shape[0]
    bp, seq, _ = x_prompt.shape
    bs, dec, _ = x_sample.shape
    past = cache_sb_k.shape[2]
    n_mem = mem_prompt.shape[1]
    assert bp == 1, "the prompt group is one stream"
    alpha = (2.0 * depth) ** 0.25
    tp, ts = seq * bp, bs * dec

    xp = x_prompt.reshape(tp, D_MODEL)
    xs = x_sample.reshape(ts, D_MODEL)
    outs = {k: [] for k in ("pool_p", "sbk_p", "sbv_p", "mk_p", "mv_p", "pool_s", "sbk_s", "sbv_s")}
    def layer(a, l):
        return a.reshape(a.shape[1:]) if a.shape[0] == 1 else a[l]

    for l in range(depth):
        w_in_bf = layer(w_in, l).astype(_BF)
        wp_bf = layer(w_pool, l).astype(_BF)
        scale = layer(pool_scale, l).reshape(1, POOL_WIDTH)
        wo_bf = layer(w_out, l).astype(_BF)
        wq_bf = layer(peer_wq, l).astype(_BF)
        sk1_bf = layer(peer_subkey1, l).astype(_BF)
        sk2_bf = layer(peer_subkey2, l).astype(_BF)
        pu_bf = layer(peer_u, l).astype(_BF)
        pvt_bf = jnp.swapaxes(layer(peer_v, l).reshape(N_EXPERTS // EXPERT_TILE, EXPERT_TILE, D_MODEL),
                              1, 2).astype(_BF)
        g1, b1 = layer(ln1_g, l).reshape(1, D_MODEL), layer(ln1_b, l).reshape(1, D_MODEL)
        g2, b2 = layer(ln2_g, l).reshape(1, D_MODEL), layer(ln2_b, l).reshape(1, D_MODEL)
        hist_s0 = layer(state_pool, l)

        p, q, k, kb, v, vb, qm = _project(xp, w_in_bf, _IN_PLAN, ROW_TILE)
        p3 = p.reshape(bp, seq, POOL_WIDTH)
        pool_o = _pool_mix(p3, jnp.zeros((bp, POOL_HIST, POOL_WIDTH), _F32), wp_bf, scale, 0, TOKEN_TILE)
        sb_o = _sb_prompt(q, kb, vb, SB_TILE, SB_HEADS_PER_STEP)
        mk, mv = _project(mem_prompt.reshape(bp * n_mem, D_MODEL), layer(w_mem_kv, l).astype(_BF), _MEM_PLAN,
                          ROW_TILE)
        mk = mk.reshape(bp, n_mem, MEM_HEADS, HEAD_DIM)
        mv = mv.reshape(bp, n_mem, MEM_HEADS, HEAD_DIM)
        mem_o = _mem_attend(qm.reshape(bp, seq, MEM_WIDTH), mk, mv, TOKEN_TILE)
        h_p, hb_p, pq_p = _tail(xp, pool_o.reshape(tp, POOL_WIDTH), sb_o, mem_o.reshape(tp, MEM_WIDTH),
                                wo_bf, g1, b1, wq_bf, alpha, ROW_TILE)
        outs["pool_p"].append(p3[:, seq - POOL_HIST:])
        outs["sbk_p"].append(k.reshape(bp, seq, SB_HEADS, HEAD_DIM))
        outs["sbv_p"].append(v.reshape(bp, seq, SB_HEADS, HEAD_DIM))
        outs["mk_p"].append(mk)
        outs["mv_p"].append(mv)

        p, q, k, kb, v, vb, qm = _project(xs, w_in_bf, _IN_PLAN, ROW_TILE)
        p3 = p.reshape(bs, dec, POOL_WIDTH)
        pool_o = _pool_mix(p3, hist_s0, wp_bf, scale, past, dec)
        sb_o = _sb_cached(q, kb.reshape(bs, dec, SB_WIDTH), vb.reshape(bs, dec, SB_WIDTH),
                          layer(cache_sb_k, l), layer(cache_sb_v, l), dec, SB_TILE)
        mem_o = _mem_attend(qm.reshape(bs, dec, MEM_WIDTH), layer(cache_mem_k, l), layer(cache_mem_v, l), dec)
        h_s, hb_s, pq_s = _tail(xs, pool_o.reshape(ts, POOL_WIDTH), sb_o, mem_o.reshape(ts, MEM_WIDTH),
                                wo_bf, g1, b1, wq_bf, alpha, ROW_TILE)
        hist_s = jnp.concatenate([hist_s0, p3], axis=1)[:, -POOL_HIST:]
        outs["pool_s"].append(hist_s)
        outs["sbk_s"].append(k.reshape(bs, dec, SB_HEADS, HEAD_DIM))
        outs["sbv_s"].append(v.reshape(bs, dec, SB_HEADS, HEAD_DIM))

        r2, e2, cnt, c1 = _score(pq_p, pq_s, sk1_bf, sk2_bf)
        ft_p = _experts(hb_p, pu_bf, pvt_bf, r2, e2, cnt, c1, EXPERT_TOKEN_TILE, EXPERT_TILE, 0)
        ft_s = _experts(hb_s, pu_bf, pvt_bf, r2, e2, cnt, c1, min(ts, EXPERT_TOKEN_TILE), EXPERT_TILE, tp)
        xp = _final_norm(h_p, ft_p, g2, b2, alpha, TOKEN_TILE)
        xs = _final_norm(h_s, ft_s, g2, b2, alpha, TOKEN_TILE)

    st = lambda name: jnp.stack(outs[name])
    return (xp.reshape(bp, seq, D_MODEL), xs.reshape(bs, dec, D_MODEL),
            st("pool_p"), st("sbk_p"), st("sbv_p"), st("mk_p"), st("mv_p"),
            st("pool_s"), st("sbk_s"), st("sbv_s"))
```

```python
import functools
import math

import jax
import jax.numpy as jnp
from jax import lax
from jax.experimental import pallas as pl
from jax.experimental.pallas import tpu as pltpu

_BF = jnp.bfloat16
_F32 = jnp.float32

D_MODEL = 2048
HEAD_DIM = 128
LANES = 128
POOL_WIDTH = D_MODEL // 4
POOL_WINDOWS = (2, 4, 8, 16)
POOL_GROUP_DIM = POOL_WIDTH // len(POOL_WINDOWS)
POOL_HIST = max(POOL_WINDOWS) - 1
POOL_HALO = POOL_HIST + 1
SB_WIDTH = D_MODEL // 2
SB_HEADS = SB_WIDTH // HEAD_DIM
MEM_WIDTH = D_MODEL // 4
MEM_HEADS = MEM_WIDTH // HEAD_DIM
IN_WIDTH = POOL_WIDTH + 3 * SB_WIDTH + MEM_WIDTH
PEER_HEADS = 8
N_KEYS = 128
N_EXPERTS = N_KEYS * N_KEYS
PEER_QDIM = 256
PEER_HALF = PEER_QDIM // 2
PEER_TOPK = 16
LN_EPS = 1e-5
ATT_SCALE = 1.0 / math.sqrt(HEAD_DIM)
SB_ZERO_LOG = -104.0
NOT_RANKED = 99.0

VMEM_LIMIT = 56 * 1024 * 1024

ROW_TILE = 256
TOKEN_TILE = 512
EXPERT_TILE = 512
EXPERT_TOKEN_TILE = 1024
SB_TILE = 256
SB_HEADS_PER_STEP = 4
SCORE_TILE = 512


def _params(*sem):
    return pltpu.CompilerParams(dimension_semantics=sem, vmem_limit_bytes=VMEM_LIMIT)


def _nt_dot(a, b):
    return lax.dot_general(a, b, (((1,), (1,)), ((), ())), preferred_element_type=_F32)


def _dot(a, b):
    return jnp.dot(a, b, preferred_element_type=_F32)


def _resident(shape):
    nd = len(shape)
    return pl.BlockSpec(shape, lambda *_: (0,) * nd, pipeline_mode=pl.Buffered(1))


def _proj_body(x_ref, w_ref, *o_refs, plan):
    xb = x_ref[...].astype(_BF)
    n = 0
    for a, b, dtypes in plan:
        y = _dot(xb, w_ref[:, a:b])
        for dt in dtypes:
            o_refs[n][...] = y.astype(dt)
            n += 1


def _project(x, w_bf, plan, tm):
    t, k = x.shape
    n = w_bf.shape[1]
    shapes, specs = [], []
    for a, b, dtypes in plan:
        for dt in dtypes:
            shapes.append(jax.ShapeDtypeStruct((t, b - a), dt))
            specs.append(pl.BlockSpec((tm, b - a), lambda i: (i, 0)))
    return pl.pallas_call(
        functools.partial(_proj_body, plan=plan),
        out_shape=shapes,
        grid=(t // tm,),
        in_specs=[pl.BlockSpec((tm, k), lambda i: (i, 0)), _resident((k, n))],
        out_specs=specs,
        compiler_params=_params("parallel"),
        name="project",
    )(x, w_bf)


def _pool_body(p_ref, prev_ref, hist_ref, wp_ref, sc_ref, o_ref, buf_ref, *, tt, pos0):
    i = pl.program_id(1)
    buf_ref[0:POOL_HALO, :] = jnp.where(i == 0, hist_ref[0], prev_ref[0])
    buf_ref[POOL_HALO:POOL_HALO + tt, :] = p_ref[0]
    pos = pos0 + i * tt + lax.broadcasted_iota(jnp.int32, (tt, POOL_GROUP_DIM), 0)
    for g, w in enumerate(POOL_WINDOWS):
        cols = slice(g * POOL_GROUP_DIM, (g + 1) * POOL_GROUP_DIM)
        win = buf_ref[POOL_HALO - (w - 1):POOL_HALO - (w - 1) + tt, cols]
        for back in range(w - 2, -1, -1):
            win = win + buf_ref[POOL_HALO - back:POOL_HALO - back + tt, cols]
        cnt = jnp.minimum(pos + 1, w).astype(_F32)
        pooled = win / cnt - p_ref[0, :, cols]
        mixed = _dot(pooled.astype(_BF), wp_ref[g]) * sc_ref[:, cols]
        o_ref[0, :, cols] = mixed.astype(o_ref.dtype)


def _pool_mix(p, hist, wp_bf, scale, pos0, tt):
    b, t, c = p.shape
    hist16 = jnp.pad(hist, ((0, 0), (POOL_HALO - POOL_HIST, 0), (0, 0)))
    per = tt // POOL_HALO
    return pl.pallas_call(
        functools.partial(_pool_body, tt=tt, pos0=pos0),
        out_shape=jax.ShapeDtypeStruct((b, t, c), _BF),
        grid=(b, t // tt),
        in_specs=[
            pl.BlockSpec((1, tt, c), lambda bi, i: (bi, i, 0)),
            pl.BlockSpec((1, POOL_HALO, c), lambda bi, i: (bi, jnp.maximum(i * per - 1, 0), 0)),
            pl.BlockSpec((1, POOL_HALO, c), lambda bi, i: (bi, 0, 0)),
            _resident(wp_bf.shape),
            _resident(scale.shape),
        ],
        out_specs=pl.BlockSpec((1, tt, c), lambda bi, i: (bi, i, 0)),
        scratch_shapes=[pltpu.VMEM((POOL_HALO + tt, c), _F32)],
        compiler_params=_params("parallel", "parallel"),
        name="pool_mix",
    )(p, p, hist16, wp_bf, scale)


def _tri_ext(tk):
    j = lax.broadcasted_iota(jnp.int32, (tk, tk + LANES), 0)
    s = lax.broadcasted_iota(jnp.int32, (tk, tk + LANES), 1)
    return jnp.where((j > s) | (s >= tk), 1.0, 0.0).astype(_BF)


def _sb_block(q, kblk, vblk, c_b, tri, mask):
    tk = kblk.shape[0]
    z = _nt_dot(q, kblk) * ATT_SCALE
    softplus = jnp.maximum(z, 0.0) + jnp.log1p(jnp.exp(-jnp.abs(z)))
    log_keep = -softplus
    if mask is not None:
        log_keep = jnp.where(mask, log_keep, 0.0)
    hi = log_keep.astype(_BF)
    lo = (log_keep - hi.astype(_F32)).astype(_BF)
    ext = _dot(hi, tri) + _dot(lo, tri)
    between, total = ext[:, :tk], ext[:, tk:]
    newer = c_b if tk == LANES else jnp.concatenate([c_b] * (tk // LANES), axis=1)
    a = jnp.exp((z - softplus) + between + newer)
    if mask is not None:
        a = jnp.where(mask, a, 0.0)
    return _dot(a.astype(_BF), vblk), c_b + total


def _causal_mask(tq, tk):
    row = lax.broadcasted_iota(jnp.int32, (tq, tk), 0)
    col = lax.broadcasted_iota(jnp.int32, (tq, tk), 1)
    return col < row


def _sb_prompt_body(q_ref, k_ref, v_ref, tri_ref, o_ref, acc_ref, c_ref, *, tq, nh):
    i = pl.program_id(1)
    tri = tri_ref[...]
    heads = [slice(h * HEAD_DIM, (h + 1) * HEAD_DIM) for h in range(nh)]

    def step(j, mask, first):
        rows = pl.ds(pl.multiple_of(j * tq, tq), tq)
        top = None
        for h, cols in enumerate(heads):
            c_b = jnp.zeros((tq, LANES), _F32) if first else c_ref[h]
            pv, c = _sb_block(q_ref[:, cols], k_ref[rows, cols], v_ref[rows, cols], c_b, tri, mask)
            acc_ref[h] = pv if first else acc_ref[h] + pv
            c_ref[h] = c
            top = jnp.max(c) if top is None else jnp.maximum(top, jnp.max(c))
        return top < SB_ZERO_LOG

    def cond(state):
        j, done = state
        return jnp.logical_and(j >= 0, jnp.logical_not(done))

    def body(state):
        j, _ = state
        return j - 1, step(j, None, False)

    lax.while_loop(cond, body, (i - 1, step(i, _causal_mask(tq, tq), True)))
    for h, cols in enumerate(heads):
        o_ref[:, cols] = acc_ref[h].astype(o_ref.dtype)


def _sb_prompt(q_bf, k_bf, v_bf, tq, nh):
    t = q_bf.shape[0]
    tri = _tri_ext(tq)
    w = nh * HEAD_DIM
    return pl.pallas_call(
        functools.partial(_sb_prompt_body, tq=tq, nh=nh),
        out_shape=jax.ShapeDtypeStruct((t, SB_WIDTH), _BF),
        grid=(SB_HEADS // nh, t // tq),
        in_specs=[
            pl.BlockSpec((tq, w), lambda g, i: (i, g)),
            pl.BlockSpec((t, w), lambda g, i: (0, g), pipeline_mode=pl.Buffered(1)),
            pl.BlockSpec((t, w), lambda g, i: (0, g), pipeline_mode=pl.Buffered(1)),
            _resident(tri.shape),
        ],
        out_specs=pl.BlockSpec((tq, w), lambda g, i: (i, g)),
        scratch_shapes=[pltpu.VMEM((nh, tq, HEAD_DIM), _F32), pltpu.VMEM((nh, tq, LANES), _F32)],
        compiler_params=_params("parallel", "parallel"),
        name="sb_prompt",
    )(q_bf, k_bf, v_bf, tri)


def _sb_cached_body(q_ref, kn_ref, vn_ref, ck_ref, cv_ref, trin_ref, tric_ref, o_ref,
                    kbuf, vbuf, sem, acc_ref, c_ref, *, td, tn, tc, n_blocks):
    b = pl.program_id(0)
    mask = _causal_mask(td, tn)
    trin = trin_ref[...]
    for h in range(SB_HEADS):
        cols = slice(h * HEAD_DIM, (h + 1) * HEAD_DIM)
        pv, c = _sb_block(q_ref[:, cols], kn_ref[0, :, cols], vn_ref[0, :, cols],
                          jnp.zeros((td, LANES), _F32), trin, mask)
        acc_ref[h] = pv
        c_ref[h] = c

    def copies(j):
        rows = pl.ds(pl.multiple_of(j * tc, tc), tc)
        return (pltpu.make_async_copy(ck_ref.at[b, rows], kbuf, sem.at[0]),
                pltpu.make_async_copy(cv_ref.at[b, rows], vbuf, sem.at[1]))

    def cond(state):
        j, done = state
        return jnp.logical_and(j >= 0, jnp.logical_not(done))

    def body(state):
        j, _ = state
        ck, cv = copies(j)
        ck.start()
        cv.start()
        ck.wait()
        cv.wait()
        tric = tric_ref[...]
        top = jnp.float32(-jnp.inf)
        for h in range(SB_HEADS):
            cols = slice(h * HEAD_DIM, (h + 1) * HEAD_DIM)
            pv, c = _sb_block(q_ref[:, cols], kbuf[:, h, :].astype(_BF), vbuf[:, h, :].astype(_BF),
                              c_ref[h], tric, None)
            acc_ref[h] += pv
            c_ref[h] = c
            top = jnp.maximum(top, jnp.max(c))
        return j - 1, top < SB_ZERO_LOG

    lax.while_loop(cond, body, (jnp.int32(n_blocks - 1), jnp.max(c_ref[...]) < SB_ZERO_LOG))
    for h in range(SB_HEADS):
        o_ref[:, h * HEAD_DIM:(h + 1) * HEAD_DIM] = acc_ref[h].astype(o_ref.dtype)


def _sb_cached(q_bf, k_new, v_new, cache_k, cache_v, td, tc):
    nb, past, heads, hd = cache_k.shape
    width = heads * hd
    tn = LANES
    pad = ((0, 0), (0, tn - td), (0, 0))
    kn, vn = jnp.pad(k_new, pad), jnp.pad(v_new, pad)
    trin, tric = _tri_ext(tn), _tri_ext(tc)
    return pl.pallas_call(
        functools.partial(_sb_cached_body, td=td, tn=tn, tc=tc, n_blocks=past // tc),
        out_shape=jax.ShapeDtypeStruct((nb * td, width), _BF),
        grid=(nb,),
        in_specs=[
            pl.BlockSpec((td, width), lambda b: (b, 0)),
            pl.BlockSpec((1, tn, width), lambda b: (b, 0, 0)),
            pl.BlockSpec((1, tn, width), lambda b: (b, 0, 0)),
            pl.BlockSpec(memory_space=pl.ANY),
            pl.BlockSpec(memory_space=pl.ANY),
            _resident(trin.shape),
            _resident(tric.shape),
        ],
        out_specs=pl.BlockSpec((td, width), lambda b: (b, 0)),
        scratch_shapes=[
            pltpu.VMEM((tc, heads, hd), _F32),
            pltpu.VMEM((tc, heads, hd), _F32),
            pltpu.SemaphoreType.DMA((2,)),
            pltpu.VMEM((SB_HEADS, td, HEAD_DIM), _F32),
            pltpu.VMEM((SB_HEADS, td, LANES), _F32),
        ],
        compiler_params=_params("arbitrary"),
        name="sb_cached",
    )(q_bf, kn, vn, cache_k, cache_v, trin, tric)


def _mem_body(q_ref, k_ref, v_ref, o_ref):
    for h in range(MEM_HEADS):
        cols = slice(h * HEAD_DIM, (h + 1) * HEAD_DIM)
        s = _nt_dot(q_ref[0, :, cols], k_ref[0, :, h, :].astype(_BF)) * ATT_SCALE
        e = jnp.exp(s - jnp.max(s, axis=1, keepdims=True))
        pr = e / jnp.sum(e, axis=1, keepdims=True)
        o_ref[0, :, cols] = _dot(pr.astype(_BF), v_ref[0, :, h, :].astype(_BF)).astype(o_ref.dtype)


def _mem_attend(qm_bf, mk, mv, tt):
    b, t, w = qm_bf.shape
    _, m, heads, hd = mk.shape
    return pl.pallas_call(
        _mem_body,
        out_shape=jax.ShapeDtypeStruct((b, t, w), _BF),
        grid=(b, t // tt),
        in_specs=[
            pl.BlockSpec((1, tt, w), lambda bi, i: (bi, i, 0)),
            pl.BlockSpec((1, m, heads, hd), lambda bi, i: (bi, 0, 0, 0)),
            pl.BlockSpec((1, m, heads, hd), lambda bi, i: (bi, 0, 0, 0)),
        ],
        out_specs=pl.BlockSpec((1, tt, w), lambda bi, i: (bi, i, 0)),
        compiler_params=_params("parallel", "parallel"),
        name="mem_attend",
    )(qm_bf, mk, mv)


def _layer_norm(r, g, b):
    mu = jnp.mean(r, axis=-1, keepdims=True)
    d = r - mu
    var = jnp.mean(d * d, axis=-1, keepdims=True)
    return d * lax.rsqrt(var + LN_EPS) * g + b


def _tail_body(x_ref, pool_ref, sb_ref, mem_ref, wo_ref, g_ref, b_ref, wq_ref, h_ref, hb_ref, pq_ref, *, alpha):
    o1, o2 = POOL_WIDTH, POOL_WIDTH + SB_WIDTH
    mix = (_dot(pool_ref[...], wo_ref[0:o1, :]) + _dot(sb_ref[...], wo_ref[o1:o2, :])
           + _dot(mem_ref[...], wo_ref[o2:, :]))
    h = _layer_norm(alpha * x_ref[...] + mix, g_ref[...], b_ref[...])
    h_ref[...] = h
    hb = h.astype(_BF)
    hb_ref[...] = hb
    for hd in range(PEER_HEADS):
        pq_ref[hd] = _dot(hb, wq_ref[:, hd * PEER_QDIM:(hd + 1) * PEER_QDIM]).astype(_BF)


def _tail(x, pool_o, sb_o, mem_o, wo_bf, g, b, wq_bf, alpha, tm):
    t = x.shape[0]
    row = lambda width: pl.BlockSpec((tm, width), lambda i: (i, 0))
    return pl.pallas_call(
        functools.partial(_tail_body, alpha=alpha),
        out_shape=[
            jax.ShapeDtypeStruct((t, D_MODEL), _F32),
            jax.ShapeDtypeStruct((t, D_MODEL), _BF),
            jax.ShapeDtypeStruct((PEER_HEADS, t, PEER_QDIM), _BF),
        ],
        grid=(t // tm,),
        in_specs=[row(D_MODEL), row(POOL_WIDTH), row(SB_WIDTH), row(MEM_WIDTH),
                  _resident(wo_bf.shape), _resident(g.shape), _resident(b.shape), _resident(wq_bf.shape)],
        out_specs=[row(D_MODEL), row(D_MODEL),
                   pl.BlockSpec((PEER_HEADS, tm, PEER_QDIM), lambda i: (0, i, 0))],
        compiler_params=_params("parallel"),
        name="tail",
    )(x, pool_o, sb_o, mem_o, wo_bf, g, b, wq_bf)


def _top_ranked(x, k):
    rows = lax.broadcasted_iota(jnp.int32, x.shape, 0).astype(_F32)
    work = x
    rank = jnp.full(x.shape, NOT_RANKED, _F32)
    vals = []
    for r in range(k):
        m = jnp.max(work, axis=0, keepdims=True)
        first = jnp.min(jnp.where(work == m, rows, float(x.shape[0])), axis=0, keepdims=True)
        sel = rows == first
        rank = jnp.where(sel, float(r), rank)
        work = jnp.where(sel, -jnp.inf, work)
        vals.append(m)
    return rank, jnp.concatenate(vals, axis=0)


def _top_ranked_untied(x, k, with_rank):
    work = x
    rank = jnp.full(x.shape, NOT_RANKED, _F32) if with_rank else None
    vals = []
    for r in range(k):
        m = jnp.max(work, axis=0, keepdims=True)
        sel = work == m
        if with_rank:
            rank = jnp.where(sel, float(r), rank)
        work = jnp.where(sel, -jnp.inf, work)
        vals.append(m)
    removed = jnp.sum(jnp.where(work == -jnp.inf, 1.0, 0.0), axis=0, keepdims=True)
    return rank, jnp.concatenate(vals, axis=0), jnp.max(jnp.abs(removed - float(k))) == 0.0


_PAIR_ROW_LEN = tuple(PEER_TOPK // (a + 1) for a in range(PEER_TOPK))
_PAIR_PAD = -sum(_PAIR_ROW_LEN) % 8


def _score_body(pqp_ref, pqs_ref, sk1_ref, sk2_ref, r2_ref, e2_ref, cnt_ref, c1_ref,
                key1_s, ref1_s, t1_s, rank2_s, t2_s, picked_s, z_s, *, prompt_steps):
    k = PEER_TOPK
    tw = r2_ref.shape[2]
    crow = lax.broadcasted_iota(jnp.int32, (sum(_PAIR_ROW_LEN) + _PAIR_PAD, tw), 0).astype(_F32)
    from_prompt = pl.program_id(0) < prompt_steps

    def head(h, carry):
        pq = jnp.where(from_prompt, pqp_ref[h], pqs_ref[h])
        s1 = _nt_dot(sk1_ref[h], pq[:, :PEER_HALF])
        s2 = _nt_dot(sk2_ref[h], pq[:, PEER_HALF:])
        _, t1_s[...], untied1 = _top_ranked_untied(s1, k, False)
        key1_s[...] = s1
        ref1_s[...] = t1_s[...]
        rank2_s[...], t2_s[...], untied2 = _top_ranked_untied(s2, k, True)

        @pl.when(jnp.logical_not(jnp.logical_and(untied1, untied2)))
        def _():
            key1_s[...], t1_s[...] = _top_ranked(s1, k)
            ref1_s[...] = lax.broadcasted_iota(jnp.int32, (k, tw), 0).astype(_F32)
            rank2_s[...], t2_s[...] = _top_ranked(s2, k)

        key1, ref1, t1, rank2, t2 = key1_s[...], ref1_s[...], t1_s[...], rank2_s[...], t2_s[...]
        cand = jnp.concatenate(
            [t1[a:a + 1] + t2[0:n] for a, n in enumerate(_PAIR_ROW_LEN)]
            + [jnp.full((_PAIR_PAD, tw), -jnp.inf, _F32)], axis=0)
        best0 = t1[0:1] + t2[0:1]

        def pick(work, tie_rule):
            picked = jnp.zeros(work.shape, _F32)
            z = jnp.zeros((1, tw), _F32)
            for _ in range(k):
                m = jnp.max(work, axis=0, keepdims=True)
                hit = work == m
                if tie_rule:
                    first = jnp.min(jnp.where(hit, crow, float(work.shape[0])), axis=0, keepdims=True)
                    hit = crow == first
                picked = jnp.where(hit, 1.0, picked)
                work = jnp.where(hit, -jnp.inf, work)
                z = z + jnp.exp(m - best0)
            return picked, z

        picked_s[...], z_s[...] = pick(cand, False)
        n_picked = jnp.sum(picked_s[...], axis=0, keepdims=True)

        @pl.when(jnp.max(jnp.abs(n_picked - float(k))) != 0.0)
        def _():
            picked_s[...], z_s[...] = pick(cand, True)

        picked, z = picked_s[...], z_s[...]
        cnt = jnp.zeros((N_KEYS, tw), _F32)
        start = 0
        for a, n in enumerate(_PAIR_ROW_LEN):
            taken = jnp.sum(picked[start:start + n], axis=0, keepdims=True)
            cnt = jnp.where(key1 == ref1[a:a + 1], taken, cnt)
            start += n
        r2_ref[h] = rank2.astype(r2_ref.dtype)
        e2_ref[h] = jnp.exp(s2 - t2[0:1]).astype(e2_ref.dtype)
        cnt_ref[h] = cnt
        c1_ref[h] = jnp.exp(s1 - t1[0:1]) / z
        return carry

    lax.fori_loop(0, PEER_HEADS, head, 0)


def _score(pq_p, pq_s, sk1_bf, sk2_bf):
    tp, ts = pq_p.shape[1], pq_s.shape[1]
    t = tp + ts
    tw = SCORE_TILE
    np_ = tp // tw
    out = lambda dt: jax.ShapeDtypeStruct((PEER_HEADS, N_KEYS, t), dt)
    spec = pl.BlockSpec((PEER_HEADS, N_KEYS, tw), lambda i: (0, 0, i))
    return pl.pallas_call(
        functools.partial(_score_body, prompt_steps=np_),
        out_shape=[out(_BF), out(_BF), out(_F32), out(_F32)],
        grid=(t // tw,),
        in_specs=[pl.BlockSpec((PEER_HEADS, tw, PEER_QDIM), lambda i: (0, jnp.minimum(i, np_ - 1), 0)),
                  pl.BlockSpec((PEER_HEADS, tw, PEER_QDIM), lambda i: (0, jnp.maximum(i - np_, 0), 0)),
                  _resident(sk1_bf.shape), _resident(sk2_bf.shape)],
        out_specs=[spec] * 4,
        scratch_shapes=[pltpu.VMEM((N_KEYS, tw), _F32), pltpu.VMEM((PEER_TOPK, tw), _F32),
                        pltpu.VMEM((PEER_TOPK, tw), _F32),
                        pltpu.VMEM((N_KEYS, tw), _F32), pltpu.VMEM((PEER_TOPK, tw), _F32),
                        pltpu.VMEM((sum(_PAIR_ROW_LEN) + _PAIR_PAD, tw), _F32), pltpu.VMEM((1, tw), _F32)],
        compiler_params=_params("parallel"),
        name="peer_score",
    )(pq_p, pq_s, sk1_bf, sk2_bf)


def _gelu(a):
    return 0.5 * a * (1.0 + lax.erf(a * (1.0 / math.sqrt(2.0))))


def _activation_tile(pu_rows, hb_ref, act_ref):
    act_ref[...] = _gelu(_nt_dot(pu_rows, hb_ref[...])).astype(_BF)


def _gated_tile(act_ref, r2_ref, e2_ref, cnt_ref, c1_ref, tile):
    te, tm = act_ref.shape
    zero = jnp.zeros((), _BF)
    out = []
    for s in range(te // N_KEYS):
        i1 = tile * (te // N_KEYS) + s
        gate = jnp.zeros((N_KEYS, tm), _BF)
        for h in range(PEER_HEADS):
            thr = cnt_ref[h, pl.ds(i1, 1), :].astype(_BF)
            c1 = c1_ref[h, pl.ds(i1, 1), :].astype(_BF)
            gate = gate + jnp.where(r2_ref[h] < thr, e2_ref[h] * c1, zero)
        out.append(gate * act_ref[s * N_KEYS:(s + 1) * N_KEYS, :])
    return jnp.concatenate(out, axis=0)


def _expert_body(hb_ref, pu_ref, pvp_ref, pvc_ref, r2_ref, e2_ref, cnt_ref, c1_ref, o_ref,
                 acta_ref, actb_ref, *, te, pairs):
    jj = pl.program_id(1)
    aux = (r2_ref, e2_ref, cnt_ref, c1_ref)

    @pl.when(jj == 0)
    def _():
        _activation_tile(pu_ref[0:te, :], hb_ref, acta_ref)
        _activation_tile(pu_ref[te:2 * te, :], hb_ref, actb_ref)
        o_ref[...] = _dot(pvc_ref[...], _gated_tile(acta_ref, *aux, 0))

    @pl.when(jnp.logical_and(jj > 0, jj < pairs))
    def _():
        w_prev = _gated_tile(actb_ref, *aux, 2 * jj - 1)
        _activation_tile(pu_ref[0:te, :], hb_ref, acta_ref)
        o_ref[...] += _dot(pvp_ref[...], w_prev)
        w_cur = _gated_tile(acta_ref, *aux, 2 * jj)
        _activation_tile(pu_ref[te:2 * te, :], hb_ref, actb_ref)
        o_ref[...] += _dot(pvc_ref[...], w_cur)

    @pl.when(jj == pairs)
    def _():
        o_ref[...] += _dot(pvp_ref[...], _gated_tile(actb_ref, *aux, 2 * pairs - 1))


def _experts(hb, pu_bf, pvt_bf, r2, e2, cnt, c1, tm, te, token0):
    t = hb.shape[0]
    tiles = N_EXPERTS // te
    pairs = tiles // 2
    off = token0 // tm
    aux = pl.BlockSpec((PEER_HEADS, N_KEYS, tm), lambda i, j: (0, 0, i + off), pipeline_mode=pl.Buffered(1))
    return pl.pallas_call(
        functools.partial(_expert_body, te=te, pairs=pairs),
        out_shape=jax.ShapeDtypeStruct((D_MODEL, t), _F32),
        grid=(t // tm, pairs + 1),
        in_specs=[
            pl.BlockSpec((tm, D_MODEL), lambda i, j: (i, 0), pipeline_mode=pl.Buffered(1)),
            pl.BlockSpec((2 * te, D_MODEL), lambda i, j: (jnp.minimum(j, pairs - 1), 0)),
            pl.BlockSpec((None, D_MODEL, te), lambda i, j: (jnp.maximum(2 * j - 1, 0), 0, 0)),
            pl.BlockSpec((None, D_MODEL, te), lambda i, j: (jnp.minimum(2 * j, tiles - 1), 0, 0)),
            aux, aux, aux, aux,
        ],
        out_specs=pl.BlockSpec((D_MODEL, tm), lambda i, j: (0, i)),
        scratch_shapes=[pltpu.VMEM((te, tm), _BF), pltpu.VMEM((te, tm), _BF)],
        compiler_params=_params("parallel", "arbitrary"),
        name="peer_experts",
    )(hb, pu_bf, pvt_bf, pvt_bf, r2, e2, cnt, c1)


def _final_body(h_ref, ft_ref, g_ref, b_ref, o_ref, *, alpha):
    o_ref[...] = _layer_norm(alpha * h_ref[...] + ft_ref[...].T, g_ref[...], b_ref[...])


def _final_norm(h, f_t, g, b, alpha, tm):
    t = h.shape[0]
    return pl.pallas_call(
        functools.partial(_final_body, alpha=alpha),
        out_shape=jax.ShapeDtypeStruct((t, D_MODEL), _F32),
        grid=(t // tm,),
        in_specs=[pl.BlockSpec((tm, D_MODEL), lambda i: (i, 0)),
                  pl.BlockSpec((D_MODEL, tm), lambda i: (0, i)),
                  _resident(g.shape), _resident(b.shape)],
        out_specs=pl.BlockSpec((tm, D_MODEL), lambda i: (i, 0)),
        compiler_params=_params("parallel"),
        name="final_norm",
    )(h, f_t, g, b)


_IN_PLAN = (
    (0, POOL_WIDTH, (_F32,)),
    (POOL_WIDTH, POOL_WIDTH + SB_WIDTH, (_BF,)),
    (POOL_WIDTH + SB_WIDTH, POOL_WIDTH + 2 * SB_WIDTH, (_F32, _BF)),
    (POOL_WIDTH + 2 * SB_WIDTH, POOL_WIDTH + 3 * SB_WIDTH, (_F32, _BF)),
    (POOL_WIDTH + 3 * SB_WIDTH, IN_WIDTH, (_BF,)),
)
_MEM_PLAN = ((0, MEM_WIDTH, (_F32,)), (MEM_WIDTH, 2 * MEM_WIDTH, (_F32,)))


def kernel(x_prompt, x_sample, state_pool, cache_sb_k, cache_sb_v, cache_mem_k, cache_mem_v, mem_prompt, w_in, w_pool, pool_scale, w_mem_kv, w_out, ln1_g, ln1_b, peer_wq, peer_subkey1, peer_subkey2, peer_u, peer_v, ln2_g, ln2_b):
    depth = w_in.shape[0]
    bp, seq, _ = x_prompt.shape
    bs, dec, _ = x_sample.shape
    past = cache_sb_k.shape[2]
    n_mem = mem_prompt.shape[1]
    assert bp == 1, "the prompt group is one stream"
    alpha = (2.0 * depth) ** 0.25
    tp, ts = seq * bp, bs * dec

    xp = x_prompt.reshape(tp, D_MODEL)
    xs = x_sample.reshape(ts, D_MODEL)
    outs = {k: [] for k in ("pool_p", "sbk_p", "sbv_p", "mk_p", "mv_p", "pool_s", "sbk_s", "sbv_s")}
    def layer(a, l):
        return a.reshape(a.shape[1:]) if a.shape[0] == 1 else a[l]

    for l in range(depth):
        w_in_bf = layer(w_in, l).astype(_BF)
        wp_bf = layer(w_pool, l).astype(_BF)
        scale = layer(pool_scale, l).reshape(1, POOL_WIDTH)
        wo_bf = layer(w_out, l).astype(_BF)
        wq_bf = layer(peer_wq, l).astype(_BF)
        sk1_bf = layer(peer_subkey1, l).astype(_BF)
        sk2_bf = layer(peer_subkey2, l).astype(_BF)
        pu_bf = layer(peer_u, l).astype(_BF)
        pvt_bf = jnp.swapaxes(layer(peer_v, l).reshape(N_EXPERTS // EXPERT_TILE, EXPERT_TILE, D_MODEL),
                              1, 2).astype(_BF)
        g1, b1 = layer(ln1_g, l).reshape(1, D_MODEL), layer(ln1_b, l).reshape(1, D_MODEL)
        g2, b2 = layer(ln2_g, l).reshape(1, D_MODEL), layer(ln2_b, l).reshape(1, D_MODEL)
        hist_s0 = layer(state_pool, l)

        p, q, k, kb, v, vb, qm = _project(xp, w_in_bf, _IN_PLAN, ROW_TILE)
        p3 = p.reshape(bp, seq, POOL_WIDTH)
        pool_o = _pool_mix(p3, jnp.zeros((bp, POOL_HIST, POOL_WIDTH), _F32), wp_bf, scale, 0, TOKEN_TILE)
        sb_o = _sb_prompt(q, kb, vb, SB_TILE, SB_HEADS_PER_STEP)
        mk, mv = _project(mem_prompt.reshape(bp * n_mem, D_MODEL), layer(w_mem_kv, l).astype(_BF), _MEM_PLAN,
                          ROW_TILE)
        mk = mk.reshape(bp, n_mem, MEM_HEADS, HEAD_DIM)
        mv = mv.reshape(bp, n_mem, MEM_HEADS, HEAD_DIM)
        mem_o = _mem_attend(qm.reshape(bp, seq, MEM_WIDTH), mk, mv, TOKEN_TILE)
        h_p, hb_p, pq_p = _tail(xp, pool_o.reshape(tp, POOL_WIDTH), sb_o, mem_o.reshape(tp, MEM_WIDTH),
                                wo_bf, g1, b1, wq_bf, alpha, ROW_TILE)
        outs["pool_p"].append(p3[:, seq - POOL_HIST:])
        outs["sbk_p"].append(k.reshape(bp, seq, SB_HEADS, HEAD_DIM))
        outs["sbv_p"].append(v.reshape(bp, seq, SB_HEADS, HEAD_DIM))
        outs["mk_p"].append(mk)
        outs["mv_p"].append(mv)

        p, q, k, kb, v, vb, qm = _project(xs, w_in_bf, _IN_PLAN, ROW_TILE)
        p3 = p.reshape(bs, dec, POOL_WIDTH)
        pool_o = _pool_mix(p3, hist_s0, wp_bf, scale, past, dec)
        sb_o = _sb_cached(q, kb.reshape(bs, dec, SB_WIDTH), vb.reshape(bs, dec, SB_WIDTH),
                          layer(cache_sb_k, l), layer(cache_sb_v, l), dec, SB_TILE)
        mem_o = _mem_attend(qm.reshape(bs, dec, MEM_WIDTH), layer(cache_mem_k, l), layer(cache_mem_v, l), dec)
        h_s, hb_s, pq_s = _tail(xs, pool_o.reshape(ts, POOL_WIDTH), sb_o, mem_o.reshape(ts, MEM_WIDTH),
                                wo_bf, g1, b1, wq_bf, alpha, ROW_TILE)
        hist_s = jnp.concatenate([hist_s0, p3], axis=1)[:, -POOL_HIST:]
        outs["pool_s"].append(hist_s)
        outs["sbk_s"].append(k.reshape(bs, dec, SB_HEADS, HEAD_DIM))
        outs["sbv_s"].append(v.reshape(bs, dec, SB_HEADS, HEAD_DIM))

        r2, e2, cnt, c1 = _score(pq_p, pq_s, sk1_bf, sk2_bf)
        ft_p = _experts(hb_p, pu_bf, pvt_bf, r2, e2, cnt, c1, EXPERT_TOKEN_TILE, EXPERT_TILE, 0)
        ft_s = _experts(hb_s, pu_bf, pvt_bf, r2, e2, cnt, c1, min(ts, EXPERT_TOKEN_TILE), EXPERT_TILE, tp)
        xp = _final_norm(h_p, ft_p, g2, b2, alpha, TOKEN_TILE)
        xs = _final_norm(h_s, ft_s, g2, b2, alpha, TOKEN_TILE)

    st = lambda name: jnp.stack(outs[name])
    return (xp.reshape(bp, seq, D_MODEL), xs.reshape(bs, dec, D_MODEL),
            st("pool_p"), st("sbk_p"), st("sbv_p"), st("mk_p"), st("mv_p"),
            st("pool_s"), st("sbk_s"), st("sbv_s"))
```
